```python
import jax, jax.numpy as jnp
from jax import lax
import numpy as np

D_MODEL = 2048
BATCH = 4
SEQ = 2048
DEPTH = 1

HEAD_DIM = 128
ATTN_WIDTH = D_MODEL // 2
ATTN_HEADS = ATTN_WIDTH // HEAD_DIM
CONV_WIDTH = D_MODEL - ATTN_WIDTH
CONV_GROUP_DIM = 128
CONV_GROUPS = CONV_WIDTH // CONV_GROUP_DIM
MIX_WIDTH = ATTN_WIDTH + CONV_WIDTH
IN_WIDTH = 3 * ATTN_WIDTH + 3 * CONV_WIDTH
CONV_K = 3
MOBA_BLOCK = 256
MOBA_TOPK = 3
Q_CHUNK = 16
N_EXPERTS = 256
TOP_K = 8
N_GROUPS = 8
TOPK_GROUPS = 4
EXPERT_DIM = 512
SHARED_DIM = 512
ROUTED_SCALE = 2.5
MOE_BLOCK = 128
EPS = 1e-6

kernel_name = "hybrid_moba_shortconv_moe_adaln"


def rms_norm(x, gain):
    xf = x.astype(jnp.float32)
    y = xf * lax.rsqrt(jnp.mean(xf * xf, axis=-1, keepdims=True) + EPS)
    return (y * gain.astype(jnp.float32)).astype(x.dtype)


def moba_attention(q, k, v):
    b, h, s, dh = q.shape
    nb = -(-s // MOBA_BLOCK)
    s_pad = nb * MOBA_BLOCK
    pad = ((0, 0), (0, 0), (0, s_pad - s), (0, 0))
    kb = jnp.pad(k, pad).reshape(b, h, nb, MOBA_BLOCK, dh)
    vb = jnp.pad(v, pad).reshape(b, h, nb, MOBA_BLOCK, dh)
    scale = dh ** -0.5
    k_mean = jnp.mean(kb.astype(jnp.float32), axis=3)
    gate = jnp.einsum('bhsd,bhnd->bhsn', q.astype(jnp.float32), k_mean)
    q_blk = jnp.arange(s) // MOBA_BLOCK
    past = jnp.arange(nb)[None, :] < q_blk[:, None]
    gate = jnp.where(past, gate, -jnp.inf)
    n_sel = min(MOBA_TOPK, nb)
    _, sel = lax.top_k(gate, n_sel)
    sel_valid = sel < q_blk[:, None]
    bi = jnp.arange(b)[:, None, None, None]
    hi = jnp.arange(h)[None, :, None, None]

    def chunk(c0):
        qc = lax.dynamic_slice_in_dim(q, c0, Q_CHUNK, axis=2)
        selc = lax.dynamic_slice_in_dim(sel, c0, Q_CHUNK, axis=2)
        validc = lax.dynamic_slice_in_dim(sel_valid, c0, Q_CHUNK, axis=2)
        own = c0 // MOBA_BLOCK
        k_own = lax.dynamic_index_in_dim(kb, own, axis=2, keepdims=False)
        v_own = lax.dynamic_index_in_dim(vb, own, axis=2, keepdims=False)
        q_pos = c0 + jnp.arange(Q_CHUNK)
        k_pos = own * MOBA_BLOCK + jnp.arange(MOBA_BLOCK)
        causal = k_pos[None, :] <= q_pos[:, None]
        s_own = jnp.einsum('bhqd,bhkd->bhqk', qc, k_own).astype(jnp.float32) * scale
        s_own = jnp.where(causal, s_own, -jnp.inf)
        k_sel = kb[bi, hi, selc]
        v_sel = vb[bi, hi, selc]
        s_sel = jnp.einsum('bhqd,bhqnkd->bhqnk', qc, k_sel).astype(jnp.float32) * scale
        s_sel = jnp.where(validc[..., None], s_sel, -jnp.inf)
        scores = jnp.concatenate([s_own, s_sel.reshape(b, h, Q_CHUNK, n_sel * MOBA_BLOCK)], axis=-1)
        p = jax.nn.softmax(scores, axis=-1)
        p_own = p[..., :MOBA_BLOCK].astype(v.dtype)
        p_sel = p[..., MOBA_BLOCK:].reshape(b, h, Q_CHUNK, n_sel, MOBA_BLOCK).astype(v.dtype)
        return (jnp.einsum('bhqk,bhkd->bhqd', p_own, v_own)
                + jnp.einsum('bhqnk,bhqnkd->bhqd', p_sel, v_sel))

    n_chunks = s // Q_CHUNK
    outs = lax.map(chunk, jnp.arange(n_chunks) * Q_CHUNK)
    return outs.transpose(1, 2, 0, 3, 4).reshape(b, h, s, dh)


def short_conv_mixer(u, b_gate, c_gate, conv_w):
    z = c_gate * u
    y = lax.conv_general_dilated(z, conv_w[:, None, :].astype(z.dtype), window_strides=(1,),
                                 padding=[(CONV_K - 1, 0)],
                                 dimension_numbers=('NWC', 'WIO', 'NWC'),
                                 feature_group_count=z.shape[-1])
    return b_gate * y


def route(h, w_router, router_bias):
    n = h.shape[0]
    scores = jax.nn.sigmoid((h @ w_router).astype(jnp.float32))
    choice = scores + router_bias.astype(jnp.float32)
    grp = choice.reshape(n, N_GROUPS, N_EXPERTS // N_GROUPS)
    grp_score = jnp.sum(lax.top_k(grp, 2)[0], axis=-1)
    _, top_g = lax.top_k(grp_score, TOPK_GROUPS)
    gmask = jnp.any(top_g[..., None] == jnp.arange(N_GROUPS), axis=-2)
    emask = jnp.repeat(gmask, N_EXPERTS // N_GROUPS, axis=-1)
    _, idx = lax.top_k(jnp.where(emask, choice, -jnp.inf), TOP_K)
    w = jnp.take_along_axis(scores, idx, axis=-1)
    w = w / jnp.sum(w, axis=-1, keepdims=True) * ROUTED_SCALE
    return idx, w


def routed_experts(h, idx, w, w_gate, w_up, w_down):
    n, d = h.shape
    e = w_gate.shape[0]
    nk = n * TOP_K
    flat_e = idx.reshape(-1)
    order = jnp.argsort(flat_e)
    e_sorted = flat_e[order]
    tok_sorted = (order // TOP_K).astype(jnp.int32)
    w_sorted = w.reshape(-1)[order]
    counts = jnp.bincount(flat_e, length=e)
    blocks_per_e = (counts + MOBA_BLOCK * 0 + MOE_BLOCK - 1) // MOE_BLOCK
    blk_end = jnp.cumsum(blocks_per_e)
    blk_start = blk_end - blocks_per_e
    grp_start = jnp.cumsum(counts) - counts
    slot = blk_start[e_sorted] * MOE_BLOCK + (jnp.arange(nk) - grp_start[e_sorted])
    n_blocks = -(-nk // MOE_BLOCK) + e
    slot_tok = jnp.full((n_blocks * MOE_BLOCK,), n, jnp.int32).at[slot].set(tok_sorted)
    slot_w = jnp.zeros((n_blocks * MOE_BLOCK,), jnp.float32).at[slot].set(w_sorted)
    blk_expert = jnp.minimum(jnp.searchsorted(blk_end, jnp.arange(n_blocks), side='right'), e - 1)
    h_pad = jnp.concatenate([h, jnp.zeros((1, d), h.dtype)], axis=0)

    def step(acc, xs):
        toks, ws, ex = xs
        xb = h_pad[toks]
        hid = jax.nn.silu(xb @ w_gate[ex]) * (xb @ w_up[ex])
        yb = (hid @ w_down[ex]).astype(jnp.float32) * ws[:, None]
        return acc.at[toks].add(yb), None

    acc, _ = lax.scan(step, jnp.zeros((n + 1, d), jnp.float32),
                      (slot_tok.reshape(n_blocks, MOE_BLOCK), slot_w.reshape(n_blocks, MOE_BLOCK), blk_expert))
    return acc[:n].astype(h.dtype)


def setup_inputs(seed: int = 0) -> dict:
    key = jax.random.key(seed)
    ks = jax.random.split(key, 24)
    f32 = jnp.float32
    nrm = lambda k, shape, s: jax.random.normal(k, shape, f32) * s
    L, D = DEPTH, D_MODEL
    return {
        "x": nrm(ks[0], (BATCH, SEQ, D), 1.0),
        "c": nrm(ks[1], (BATCH, D), 1.0),
        "w_ada": nrm(ks[2], (L, D, 6 * D), 0.5 * D ** -0.5),
        "b_ada": nrm(ks[3], (L, 6 * D), 0.01),
        "norm1_g": 1.0 + nrm(ks[4], (L, D), 0.02),
        "w_in": nrm(ks[5], (L, D, IN_WIDTH), D ** -0.5),
        "q_norm_g": 1.0 + nrm(ks[6], (L, HEAD_DIM), 0.02),
        "k_norm_g": 1.0 + nrm(ks[7], (L, HEAD_DIM), 0.02),
        "conv_w": nrm(ks[8], (L, CONV_K, CONV_WIDTH), CONV_K ** -0.5),
        "attn_out_g": 1.0 + nrm(ks[9], (L, ATTN_WIDTH), 0.02),
        "conv_out_g": 1.0 + nrm(ks[10], (L, CONV_WIDTH), 0.02),
        "w_o": nrm(ks[11], (L, MIX_WIDTH, D), MIX_WIDTH ** -0.5),
        "norm2_g": 1.0 + nrm(ks[12], (L, D), 0.02),
        "w_router": nrm(ks[13], (L, D, N_EXPERTS), D ** -0.5),
        "router_bias": nrm(ks[14], (L, N_EXPERTS), 0.01),
        "w_gate": nrm(ks[15], (L, N_EXPERTS, D, EXPERT_DIM), D ** -0.5),
        "w_up": nrm(ks[16], (L, N_EXPERTS, D, EXPERT_DIM), D ** -0.5),
        "w_down": nrm(ks[17], (L, N_EXPERTS, EXPERT_DIM, D), EXPERT_DIM ** -0.5),
        "ws_gate": nrm(ks[18], (L, D, SHARED_DIM), D ** -0.5),
        "ws_up": nrm(ks[19], (L, D, SHARED_DIM), D ** -0.5),
        "ws_down": nrm(ks[20], (L, SHARED_DIM, D), SHARED_DIM ** -0.5),
    }


def reference(x, c, w_ada, b_ada, norm1_g, w_in, q_norm_g, k_norm_g, conv_w, attn_out_g,
              conv_out_g, w_o, norm2_g, w_router, router_bias, w_gate, w_up, w_down,
              ws_gate, ws_up, ws_down):
    b, s, d = x.shape
    A, Cw = ATTN_WIDTH, CONV_WIDTH
    for l in range(DEPTH):
        mod = (jax.nn.silu(c) @ w_ada[l] + b_ada[l])[:, None, :]
        sh1, sc1, g1, sh2, sc2, g2 = jnp.split(mod, 6, axis=-1)

        hmix = rms_norm(x, norm1_g[l]) * (1.0 + sc1) + sh1
        proj = hmix @ w_in[l]
        q, k, v, u, bg, cg = jnp.split(proj, [A, 2 * A, 3 * A, 3 * A + Cw, 3 * A + 2 * Cw], axis=-1)
        q = rms_norm(q.reshape(b, s, ATTN_HEADS, HEAD_DIM), q_norm_g[l]).transpose(0, 2, 1, 3)
        k = rms_norm(k.reshape(b, s, ATTN_HEADS, HEAD_DIM), k_norm_g[l]).transpose(0, 2, 1, 3)
        v = v.reshape(b, s, ATTN_HEADS, HEAD_DIM).transpose(0, 2, 1, 3)
        a = moba_attention(q, k, v)
        a = rms_norm(a, attn_out_g[l].reshape(ATTN_HEADS, 1, HEAD_DIM))
        a = a.transpose(0, 2, 1, 3).reshape(b, s, A)
        y = short_conv_mixer(u, bg, cg, conv_w[l])
        y = rms_norm(y.reshape(b, s, CONV_GROUPS, CONV_GROUP_DIM),
                     conv_out_g[l].reshape(CONV_GROUPS, CONV_GROUP_DIM)).reshape(b, s, Cw)
        mix = jnp.concatenate([a, y], axis=-1) @ w_o[l]
        x = x + g1 * mix

        hf = (rms_norm(x, norm2_g[l]) * (1.0 + sc2) + sh2).reshape(b * s, d)
        idx, wts = route(hf, w_router[l], router_bias[l])
        routed = routed_experts(hf, idx, wts, w_gate[l], w_up[l], w_down[l])
        shared = (jax.nn.silu(hf @ ws_gate[l]) * (hf @ ws_up[l])) @ ws_down[l]
        x = x + g2 * (routed + shared).reshape(b, s, d)
    return x
```

```python
import functools

import jax
import jax.numpy as jnp
from jax import lax
from jax.experimental import pallas as pl
from jax.experimental.pallas import tpu as pltpu

F32 = jnp.float32
BF16 = jnp.bfloat16
I32 = jnp.int32

D_MODEL = 2048
BATCH = 4
SEQ = 2048
N_TOK = BATCH * SEQ
HEAD_DIM = 128
ATTN_WIDTH = 1024
ATTN_HEADS = 8
CONV_WIDTH = 1024
CONV_GROUPS = 8
IN_WIDTH = 6144
CONV_K = 3
MOBA_BLOCK = 256
MOBA_NB = SEQ // MOBA_BLOCK
MOBA_TOPK = 3
N_EXPERTS = 256
TOP_K = 8
N_GROUPS = 8
GROUP_SIZE = N_EXPERTS // N_GROUPS
TOPK_GROUPS = 4
EXPERT_DIM = 512
SHARED_DIM = 512
ROUTED_SCALE = 2.5
EPS = 1e-6

LANES = 128
MASKED = -1e30
VMEM_LIMIT = 56 * 1024 * 1024

ADA_TN = 1024
PROJ_TM = 1024
PROJ_TN = 512
ROUTE_TM = 256
DISP_TM = 256
MOE_ROWS = 128
N_SLOTS = N_TOK * TOP_K
N_ROW_BLOCKS = N_SLOTS // MOE_ROWS
MAX_ITEMS = N_ROW_BLOCKS + N_EXPERTS
COMB_TM = 128


def _dot(a, b):
    return jnp.dot(a, b, preferred_element_type=F32)


def _dot_nt(a, b):
    return lax.dot_general(a, b, (((1,), (1,)), ((), ())), preferred_element_type=F32)


def _params(sem):
    return pltpu.CompilerParams(dimension_semantics=sem, vmem_limit_bytes=VMEM_LIMIT)


def _adaln_kernel(c_ref, w_ref, b_ref, o_ref):
    c = c_ref[...]
    o_ref[...] = _dot(c * jax.nn.sigmoid(c), w_ref[...]) + b_ref[...]


def _adaln(c8, w_ada, b_ada):
    n = w_ada.shape[-1]
    return pl.pallas_call(
        _adaln_kernel,
        grid=(n // ADA_TN,),
        in_specs=[
            pl.BlockSpec((8, D_MODEL), lambda j: (0, 0)),
            pl.BlockSpec((None, D_MODEL, ADA_TN), lambda j: (0, 0, j)),
            pl.BlockSpec((1, ADA_TN), lambda j: (0, j)),
        ],
        out_specs=pl.BlockSpec((8, ADA_TN), lambda j: (0, j)),
        out_shape=jax.ShapeDtypeStruct((8, n), F32),
        compiler_params=_params(("arbitrary",)),
        name="adaln",
    )(c8, w_ada, b_ada)


def _in_proj_kernel(x_ref, mod_ref, g_ref, w_ref, qkg_ref, o_ref, h_ref):
    j = pl.program_id(1)

    @pl.when(j == 0)
    def _():
        xf = x_ref[...]
        ms = jnp.mean(xf * xf, axis=-1, keepdims=True)
        y = xf * lax.rsqrt(ms + EPS) * g_ref[...]
        h_ref[...] = (y * (1.0 + mod_ref[0, 1:2, :]) + mod_ref[0, 0:1, :]).astype(BF16)

    acc = _dot(h_ref[...], w_ref[...].astype(BF16))

    @pl.when(j < 2 * ATTN_WIDTH // PROJ_TN)
    def _():
        for hh in range(PROJ_TN // HEAD_DIM):
            sl = slice(hh * HEAD_DIM, (hh + 1) * HEAD_DIM)
            a = acc[:, sl]
            ms = jnp.mean(a * a, axis=-1, keepdims=True)
            o_ref[:, sl] = (a * lax.rsqrt(ms + EPS) * qkg_ref[:, sl]).astype(BF16)

    @pl.when(j >= 2 * ATTN_WIDTH // PROJ_TN)
    def _():
        o_ref[...] = acc.astype(BF16)


def _in_proj(x2, mod3, norm1_g, w_in, qk_gain):
    n_qk = 2 * ATTN_WIDTH // PROJ_TN
    rows_per_batch = SEQ // PROJ_TM
    return pl.pallas_call(
        _in_proj_kernel,
        grid=(N_TOK // PROJ_TM, IN_WIDTH // PROJ_TN),
        in_specs=[
            pl.BlockSpec((PROJ_TM, D_MODEL), lambda i, j: (i, 0)),
            pl.BlockSpec((1, 6, D_MODEL), lambda i, j: (i // rows_per_batch, 0, 0)),
            pl.BlockSpec((1, D_MODEL), lambda i, j: (0, 0)),
            pl.BlockSpec((None, D_MODEL, PROJ_TN), lambda i, j: (0, 0, j)),
            pl.BlockSpec((1, PROJ_TN), lambda i, j: (0, jnp.minimum(j, n_qk - 1))),
        ],
        out_specs=pl.BlockSpec((PROJ_TM, PROJ_TN), lambda i, j: (i, j)),
        out_shape=jax.ShapeDtypeStruct((N_TOK, IN_WIDTH), BF16),
        scratch_shapes=[pltpu.VMEM((PROJ_TM, D_MODEL), BF16)],
        compiler_params=_params(("arbitrary", "arbitrary")),
        name="in_proj",
    )(x2, mod3, norm1_g, w_in, qk_gain)


def _mixer_kernel(q_ref, k_ref, v_ref, u_ref, bg_ref, cg_ref, cw_ref, ag_ref, yg_ref,
                  a_ref, y_ref, s_ref):
    blk = MOBA_BLOCK
    k = k_ref[...]
    v = v_ref[...]

    km = jnp.mean(k.astype(F32).reshape(MOBA_NB, blk, HEAD_DIM), axis=1)
    km_hi = km.astype(BF16)
    km_lo = (km - km_hi.astype(F32)).astype(BF16)
    zpad = jnp.zeros((LANES - MOBA_NB, HEAD_DIM), BF16)
    km_hi = jnp.concatenate([km_hi, zpad], axis=0)
    km_lo = jnp.concatenate([km_lo, zpad], axis=0)

    lane = lax.broadcasted_iota(I32, (blk, LANES), 1)
    row = lax.broadcasted_iota(I32, (blk, blk), 0)
    col = lax.broadcasted_iota(I32, (blk, blk), 1)

    for i in range(MOBA_NB):
        qi = q_ref[i * blk:(i + 1) * blk, :]
        if i > 0:
            if i > MOBA_TOPK:
                g = _dot_nt(qi, km_hi) + _dot_nt(qi, km_lo)
                rank = jnp.zeros((blk, LANES), F32)
                for jp in range(i):
                    cj = g[:, jp:jp + 1]
                    tie = jnp.where(lane > jp, 1.0, 0.0)
                    rank = rank + jnp.where(cj > g, 1.0, jnp.where(cj == g, tie, 0.0))
                sel = jnp.where(rank < MOBA_TOPK, 1.0, 0.0)
            else:
                sel = jnp.ones((blk, LANES), F32)
            for j in range(i):
                s = _dot_nt(qi, k[j * blk:(j + 1) * blk])
                s_ref[:, j * blk:(j + 1) * blk] = jnp.where(sel[:, j:j + 1] > 0.5, s, MASKED)
        s = _dot_nt(qi, k[i * blk:(i + 1) * blk])
        s_ref[:, i * blk:(i + 1) * blk] = jnp.where(col <= row, s, MASKED)

        width = (i + 1) * blk
        sc = s_ref[:, :width]
        m = jnp.max(sc, axis=-1, keepdims=True)
        p = jnp.exp(sc - m)
        denom = jnp.sum(p, axis=-1, keepdims=True)
        o = _dot(p.astype(BF16), v[:width]) / denom
        ms = jnp.mean(o * o, axis=-1, keepdims=True)
        a_ref[i * blk:(i + 1) * blk, :] = (o * lax.rsqrt(ms + EPS) * ag_ref[...]).astype(BF16)

    z = cg_ref[...].astype(F32) * u_ref[...].astype(F32)
    t = lax.broadcasted_iota(I32, z.shape, 0)
    z1 = jnp.where(t >= 1, pltpu.roll(z, 1, 0), 0.0)
    z2 = jnp.where(t >= 2, pltpu.roll(z, 2, 0), 0.0)
    y = cw_ref[0:1, :] * z2 + cw_ref[1:2, :] * z1 + cw_ref[2:3, :] * z
    y = bg_ref[...].astype(F32) * y
    ms = jnp.mean(y * y, axis=-1, keepdims=True)
    y_ref[...] = (y * lax.rsqrt(ms + EPS) * yg_ref[...]).astype(BF16)


def _mixer(proj, conv_w, attn_out_g, conv_out_g):
    h = ATTN_HEADS

    def col(off):
        return pl.BlockSpec((SEQ, HEAD_DIM), lambda b, g: (b, off + g))

    out_spec = pl.BlockSpec((SEQ, HEAD_DIM), lambda b, g: (b, g))
    vec_spec = pl.BlockSpec((1, HEAD_DIM), lambda b, g: (0, g))
    return pl.pallas_call(
        _mixer_kernel,
        grid=(BATCH, h),
        in_specs=[col(0), col(h), col(2 * h), col(3 * h), col(4 * h), col(5 * h),
                  pl.BlockSpec((None, CONV_K, HEAD_DIM), lambda b, g: (0, 0, g)),
                  vec_spec, vec_spec],
        out_specs=[out_spec, out_spec],
        out_shape=[jax.ShapeDtypeStruct((N_TOK, ATTN_WIDTH), BF16),
                   jax.ShapeDtypeStruct((N_TOK, CONV_WIDTH), BF16)],
        scratch_shapes=[pltpu.VMEM((MOBA_BLOCK, SEQ), F32)],
        compiler_params=_params(("arbitrary", "arbitrary")),
        name="mixer",
    )(proj, proj, proj, proj, proj, proj, conv_w, attn_out_g, conv_out_g)


def _out_proj_kernel(a_ref, y_ref, wa_ref, wb_ref, x_ref, mod_ref, o_ref):
    mix = _dot(a_ref[...], wa_ref[...].astype(BF16)) + _dot(y_ref[...], wb_ref[...].astype(BF16))
    o_ref[...] = x_ref[...] + mod_ref[0, 2:3, :] * mix


def _out_proj(a, y, w_o, x2, mod3):
    rows_per_batch = SEQ // PROJ_TM
    return pl.pallas_call(
        _out_proj_kernel,
        grid=(N_TOK // PROJ_TM, D_MODEL // PROJ_TN),
        in_specs=[
            pl.BlockSpec((PROJ_TM, ATTN_WIDTH), lambda i, j: (i, 0)),
            pl.BlockSpec((PROJ_TM, CONV_WIDTH), lambda i, j: (i, 0)),
            pl.BlockSpec((None, ATTN_WIDTH, PROJ_TN), lambda i, j: (0, 0, j)),
            pl.BlockSpec((None, CONV_WIDTH, PROJ_TN), lambda i, j: (0, 1, j)),
            pl.BlockSpec((PROJ_TM, PROJ_TN), lambda i, j: (i, j)),
            pl.BlockSpec((1, 6, PROJ_TN), lambda i, j: (i // rows_per_batch, 0, j)),
        ],
        out_specs=pl.BlockSpec((PROJ_TM, PROJ_TN), lambda i, j: (i, j)),
        out_shape=jax.ShapeDtypeStruct((N_TOK, D_MODEL), F32),
        compiler_params=_params(("arbitrary", "arbitrary")),
        name="out_proj",
    )(a, y, w_o, w_o, x2, mod3)


def _route_kernel(x_ref, g_ref, mod_ref, wrt_ref, bias_ref, wsg_ref, wsu_ref, wsd_ref,
                  hf_ref, base_ref, idx_ref, wt_ref, pos_ref, cnt_ref, carry_ref):
    tm = ROUTE_TM
    ne = N_EXPERTS

    @pl.when(pl.program_id(0) == 0)
    def _():
        carry_ref[...] = jnp.zeros_like(carry_ref)

    xf = x_ref[...]
    ms = jnp.mean(xf * xf, axis=-1, keepdims=True)
    hf = xf * lax.rsqrt(ms + EPS) * g_ref[...] * (1.0 + mod_ref[0, 4:5, :]) + mod_ref[0, 3:4, :]
    hf_ref[...] = hf
    hb = hf.astype(BF16)

    hg = _dot(hb, wsg_ref[...].astype(BF16))
    hu = _dot(hb, wsu_ref[...].astype(BF16))
    hid = (hg * jax.nn.sigmoid(hg)) * hu
    shared = _dot(hid.astype(BF16), wsd_ref[...].astype(BF16))
    base_ref[...] = xf + mod_ref[0, 5:6, :] * shared

    scores = jax.nn.sigmoid(_dot_nt(wrt_ref[...].astype(BF16), hb))
    choice = scores + bias_ref[:, 0:1]
    ninf = -jnp.inf

    gi = lax.broadcasted_iota(I32, (GROUP_SIZE, tm), 0)
    rows = []
    for g in range(N_GROUPS):
        blk = choice[g * GROUP_SIZE:(g + 1) * GROUP_SIZE, :]
        m1 = jnp.max(blk, axis=0, keepdims=True)
        i1 = jnp.min(jnp.where(blk == m1, gi, GROUP_SIZE), axis=0, keepdims=True)
        m2 = jnp.max(jnp.where(gi == i1, ninf, blk), axis=0, keepdims=True)
        rows.append(m1 + m2)
    gsc = jnp.concatenate(rows, axis=0)

    gidx = lax.broadcasted_iota(I32, (N_GROUPS, tm), 0)
    rank = jnp.zeros((N_GROUPS, tm), F32)
    for gp in range(N_GROUPS):
        r = gsc[gp:gp + 1, :]
        tie = jnp.where(gidx > gp, 1.0, 0.0)
        rank = rank + jnp.where(r > gsc, 1.0, jnp.where(r == gsc, tie, 0.0))
    gsel = jnp.where(rank < TOPK_GROUPS, 1.0, 0.0)
    esel = jnp.concatenate(
        [jnp.broadcast_to(gsel[g:g + 1, :], (GROUP_SIZE, tm)) for g in range(N_GROUPS)], axis=0)
    masked = jnp.where(esel > 0.5, choice, ninf)

    eidx = lax.broadcasted_iota(I32, (ne, tm), 0)
    idx_rows, w_rows = [], []
    onehot = jnp.zeros((ne, tm), F32)
    for _ in range(TOP_K):
        m = jnp.max(masked, axis=0, keepdims=True)
        sel = jnp.min(jnp.where(masked == m, eidx, ne), axis=0, keepdims=True)
        hit = eidx == sel
        idx_rows.append(sel)
        w_rows.append(jnp.sum(jnp.where(hit, scores, 0.0), axis=0, keepdims=True))
        masked = jnp.where(hit, ninf, masked)
        onehot = jnp.where(hit, 1.0, onehot)
    wsel = jnp.concatenate(w_rows, axis=0)
    idx_ref[...] = jnp.concatenate(idx_rows, axis=0)
    wt_ref[...] = wsel / jnp.sum(wsel, axis=0, keepdims=True) * ROUTED_SCALE

    ti = lax.broadcasted_iota(I32, (tm, tm), 0)
    tj = lax.broadcasted_iota(I32, (tm, tm), 1)
    upper = jnp.where(ti < tj, 1.0, 0.0).astype(BF16)
    before = _dot(onehot.astype(BF16), upper) + carry_ref[:, 0:1]
    pos_rows = [jnp.sum(jnp.where(eidx == idx_rows[kk], before, 0.0), axis=0, keepdims=True)
                for kk in range(TOP_K)]
    pos_ref[...] = jnp.concatenate(pos_rows, axis=0).astype(I32)
    carry_ref[...] = carry_ref[...] + jnp.sum(onehot, axis=1, keepdims=True)
    cnt_ref[...] = carry_ref[...].astype(I32)


def _route(x1, norm2_g, mod3, w_rt, bias_col, ws_gate, ws_up, ws_down):
    tm = ROUTE_TM
    rows_per_batch = SEQ // tm
    full = lambda shape: pl.BlockSpec(shape, lambda i: tuple(0 for _ in shape))
    row_tile = pl.BlockSpec((tm, D_MODEL), lambda i: (i, 0))
    tok_tile = pl.BlockSpec((TOP_K, tm), lambda i: (0, i))
    return pl.pallas_call(
        _route_kernel,
        grid=(N_TOK // tm,),
        in_specs=[
            row_tile,
            full((1, D_MODEL)),
            pl.BlockSpec((1, 6, D_MODEL), lambda i: (i // rows_per_batch, 0, 0)),
            full((N_EXPERTS, D_MODEL)),
            full((N_EXPERTS, LANES)),
            pl.BlockSpec((None, D_MODEL, SHARED_DIM), lambda i: (0, 0, 0)),
            pl.BlockSpec((None, D_MODEL, SHARED_DIM), lambda i: (0, 0, 0)),
            pl.BlockSpec((None, SHARED_DIM, D_MODEL), lambda i: (0, 0, 0)),
        ],
        out_specs=[row_tile, row_tile, tok_tile, tok_tile, tok_tile, full((N_EXPERTS, LANES))],
        out_shape=[
            jax.ShapeDtypeStruct((N_TOK, D_MODEL), F32),
            jax.ShapeDtypeStruct((N_TOK, D_MODEL), F32),
            jax.ShapeDtypeStruct((TOP_K, N_TOK), I32),
            jax.ShapeDtypeStruct((TOP_K, N_TOK), F32),
            jax.ShapeDtypeStruct((TOP_K, N_TOK), I32),
            jax.ShapeDtypeStruct((N_EXPERTS, LANES), I32),
        ],
        scratch_shapes=[pltpu.VMEM((N_EXPERTS, LANES), F32)],
        compiler_params=_params(("arbitrary",)),
        name="route",
    )(x1, norm2_g, mod3, w_rt, bias_col, ws_gate, ws_up, ws_down)


def _row_copy(src_hbm, src_row, dst_hbm, dst_row, sem):
    return pltpu.make_async_copy(src_hbm.at[pl.ds(src_row, 1), :], dst_hbm.at[pl.ds(dst_row, 1), :], sem)


def _dispatch_kernel(start_ref, idx_ref, pos_ref, hf_hbm, xs_hbm, slot_ref, sem):
    tm = DISP_TM
    base = pl.program_id(0) * tm

    def issue(t, carry):
        for kk in range(TOP_K):
            slot = start_ref[idx_ref[kk, t]] + pos_ref[kk, t]
            slot_ref[kk, t] = slot
            _row_copy(hf_hbm, base + t, xs_hbm, slot, sem).start()
        return carry

    lax.fori_loop(0, tm, issue, 0)

    def drain(t, carry):
        for _ in range(TOP_K):
            _row_copy(hf_hbm, 0, xs_hbm, 0, sem).wait()
        return carry

    lax.fori_loop(0, tm, drain, 0)


def _dispatch(grp_start, idx_t, pos_t, hf):
    tm = DISP_TM
    tok_tile = pl.BlockSpec((TOP_K, tm), lambda i, st: (0, i), memory_space=pltpu.SMEM)
    grid_spec = pltpu.PrefetchScalarGridSpec(
        num_scalar_prefetch=1,
        grid=(N_TOK // tm,),
        in_specs=[tok_tile, tok_tile, pl.BlockSpec(memory_space=pl.ANY)],
        out_specs=[pl.BlockSpec(memory_space=pl.ANY), tok_tile],
        scratch_shapes=[pltpu.SemaphoreType.DMA(())],
    )
    return pl.pallas_call(
        _dispatch_kernel,
        grid_spec=grid_spec,
        out_shape=[jax.ShapeDtypeStruct((N_SLOTS, D_MODEL), F32),
                   jax.ShapeDtypeStruct((TOP_K, N_TOK), I32)],
        compiler_params=_params(("arbitrary",)),
        name="dispatch",
    )(grp_start, idx_t, pos_t, hf)


def _experts_kernel(blk_ref, exp_ref, lo_ref, hi_ref, n_ref, xs_ref, wg_ref, wu_ref, wd_ref, o_ref):
    i = pl.program_id(0)

    @pl.when(i < n_ref[0])
    def _():
        x = xs_ref[...].astype(BF16)
        hg = _dot(x, wg_ref[...].astype(BF16))
        hu = _dot(x, wu_ref[...].astype(BF16))
        hid = (hg * jax.nn.sigmoid(hg)) * hu
        y = _dot(hid.astype(BF16), wd_ref[...].astype(BF16))
        r = lax.broadcasted_iota(I32, (MOE_ROWS, 1), 0)
        lo = lo_ref[i]
        mine = (r >= lo) & (r < hi_ref[i])

        @pl.when(lo == 0)
        def _():
            o_ref[...] = jnp.where(mine, y, 0.0)

        @pl.when(lo > 0)
        def _():
            o_ref[...] = jnp.where(mine, y, o_ref[...])


def _experts(item_blk, item_exp, item_lo, item_hi, n_items, xs, w_gate, w_up, w_down):
    rows = pl.BlockSpec((MOE_ROWS, D_MODEL), lambda i, blk, ex, lo, hi, n: (blk[i], 0))
    grid_spec = pltpu.PrefetchScalarGridSpec(
        num_scalar_prefetch=5,
        grid=(MAX_ITEMS,),
        in_specs=[
            rows,
            pl.BlockSpec((None, None, D_MODEL, EXPERT_DIM), lambda i, blk, ex, lo, hi, n: (0, ex[i], 0, 0)),
            pl.BlockSpec((None, None, D_MODEL, EXPERT_DIM), lambda i, blk, ex, lo, hi, n: (0, ex[i], 0, 0)),
            pl.BlockSpec((None, None, EXPERT_DIM, D_MODEL), lambda i, blk, ex, lo, hi, n: (0, ex[i], 0, 0)),
        ],
        out_specs=rows,
    )
    return pl.pallas_call(
        _experts_kernel,
        grid_spec=grid_spec,
        out_shape=jax.ShapeDtypeStruct((N_SLOTS, D_MODEL), F32),
        compiler_params=_params(("arbitrary",)),
        name="experts",
    )(item_blk, item_exp, item_lo, item_hi, n_items, xs, w_gate, w_up, w_down)


def _combine_kernel(slot_ref, wt_ref, base_ref, mod_ref, ys_hbm, o_ref, buf_ref, sem):
    tm = COMB_TM

    def issue(t, carry):
        for kk in range(TOP_K):
            pltpu.make_async_copy(ys_hbm.at[pl.ds(slot_ref[kk, t], 1), :],
                                  buf_ref.at[kk, pl.ds(t, 1), :], sem).start()
        return carry

    lax.fori_loop(0, tm, issue, 0)

    def drain(t, carry):
        for _ in range(TOP_K):
            pltpu.make_async_copy(ys_hbm.at[pl.ds(0, 1), :], buf_ref.at[0, pl.ds(0, 1), :], sem).wait()
        return carry

    lax.fori_loop(0, tm, drain, 0)

    w_tok = wt_ref[...].T
    acc = w_tok[:, 0:1] * buf_ref[0]
    for kk in range(1, TOP_K):
        acc = acc + w_tok[:, kk:kk + 1] * buf_ref[kk]
    o_ref[...] = base_ref[...] + mod_ref[0, 5:6, :] * acc


def _combine(slot_t, w_t, base, mod3, ys):
    tm = COMB_TM
    rows_per_batch = SEQ // tm
    return pl.pallas_call(
        _combine_kernel,
        grid=(N_TOK // tm,),
        in_specs=[
            pl.BlockSpec((TOP_K, tm), lambda i: (0, i), memory_space=pltpu.SMEM),
            pl.BlockSpec((TOP_K, tm), lambda i: (0, i)),
            pl.BlockSpec((tm, D_MODEL), lambda i: (i, 0)),
            pl.BlockSpec((1, 6, D_MODEL), lambda i: (i // rows_per_batch, 0, 0)),
            pl.BlockSpec(memory_space=pl.ANY),
        ],
        out_specs=pl.BlockSpec((tm, D_MODEL), lambda i: (i, 0)),
        out_shape=jax.ShapeDtypeStruct((N_TOK, D_MODEL), F32),
        scratch_shapes=[pltpu.VMEM((TOP_K, tm, D_MODEL), F32), pltpu.SemaphoreType.DMA(())],
        compiler_params=_params(("arbitrary",)),
        name="combine",
    )(slot_t, w_t, base, mod3, ys)


def _work_items(counts):
    rows = MOE_ROWS
    grp_end = jnp.cumsum(counts)
    grp_start = grp_end - counts
    first_blk = grp_start // rows
    last_blk = (grp_end - 1) // rows
    n_blk = jnp.where(counts > 0, last_blk - first_blk + 1, 0)
    item_end = jnp.cumsum(n_blk)
    item_start = item_end - n_blk
    n_items = item_end[-1]
    i = jnp.minimum(jnp.arange(MAX_ITEMS, dtype=I32), n_items - 1)
    e = jnp.minimum(jnp.searchsorted(item_end, i, side="right"), N_EXPERTS - 1).astype(I32)
    blk = first_blk[e] + (i - item_start[e])
    lo = jnp.maximum(grp_start[e], blk * rows) - blk * rows
    hi = jnp.minimum(grp_end[e], (blk + 1) * rows) - blk * rows
    as_i32 = lambda a: a.astype(I32)
    return (as_i32(grp_start), as_i32(blk), e, as_i32(lo), as_i32(hi), as_i32(n_items).reshape(1))


def kernel(x, c, w_ada, b_ada, norm1_g, w_in, q_norm_g, k_norm_g, conv_w, attn_out_g, conv_out_g,
           w_o, norm2_g, w_router, router_bias, w_gate, w_up, w_down, ws_gate, ws_up, ws_down):
    b, s, d = x.shape
    x2 = x.reshape(b * s, d)

    c8 = jnp.pad(c, ((0, 8 - b), (0, 0)))
    mod3 = _adaln(c8, w_ada, b_ada)[:b].reshape(b, 6, d)

    qk_gain = jnp.concatenate([jnp.tile(q_norm_g[0] * HEAD_DIM ** -0.5, ATTN_HEADS),
                               jnp.tile(k_norm_g[0], ATTN_HEADS)]).reshape(1, 2 * ATTN_WIDTH)
    proj = _in_proj(x2, mod3, norm1_g, w_in, qk_gain)

    a, y = _mixer(proj, conv_w, attn_out_g, conv_out_g)
    x1 = _out_proj(a, y, w_o, x2, mod3)

    w_rt = w_router[0].T
    bias_col = jnp.broadcast_to(router_bias[0][:, None], (N_EXPERTS, LANES))
    hf, base, idx_t, w_t, pos_t, cnt = _route(x1, norm2_g, mod3, w_rt, bias_col, ws_gate, ws_up, ws_down)

    grp_start, item_blk, item_exp, item_lo, item_hi, n_items = _work_items(cnt[:, 0])
    xs, slot_t = _dispatch(grp_start, idx_t, pos_t, hf)
    ys = _experts(item_blk, item_exp, item_lo, item_hi, n_items, xs, w_gate, w_up, w_down)
    out = _combine(slot_t, w_t, base, mod3, ys)
    return out.reshape(b, s, d)
```

```python
import functools

import jax
import jax.numpy as jnp
from jax import lax
from jax.experimental import pallas as pl
from jax.experimental.pallas import tpu as pltpu

F32 = jnp.float32
BF16 = jnp.bfloat16
I32 = jnp.int32

D_MODEL = 2048
BATCH = 4
SEQ = 2048
N_TOK = BATCH * SEQ
HEAD_DIM = 128
ATTN_WIDTH = 1024
ATTN_HEADS = 8
CONV_WIDTH = 1024
CONV_GROUPS = 8
IN_WIDTH = 6144
CONV_K = 3
MOBA_BLOCK = 256
MOBA_NB = SEQ // MOBA_BLOCK
MOBA_TOPK = 3
N_EXPERTS = 256
TOP_K = 8
N_GROUPS = 8
GROUP_SIZE = N_EXPERTS // N_GROUPS
TOPK_GROUPS = 4
EXPERT_DIM = 512
SHARED_DIM = 512
ROUTED_SCALE = 2.5
EPS = 1e-6

LANES = 128
MASKED = -1e30
VMEM_LIMIT = 56 * 1024 * 1024

ADA_TN = 1024
PROJ_TM = 1024
PROJ_TN = 512
ROUTE_TM = 256
DISP_TM = 256
MOE_ROWS = 128
N_SLOTS = N_TOK * TOP_K
N_ROW_BLOCKS = N_SLOTS // MOE_ROWS
MAX_ITEMS = N_ROW_BLOCKS + N_EXPERTS
COMB_TM = 128


def _dot(a, b):
    return jnp.dot(a, b, preferred_element_type=F32)


def _dot_nt(a, b):
    return lax.dot_general(a, b, (((1,), (1,)), ((), ())), preferred_element_type=F32)


def _params(sem):
    return pltpu.CompilerParams(dimension_semantics=sem, vmem_limit_bytes=VMEM_LIMIT)


def _adaln_kernel(c_ref, w_ref, b_ref, o_ref):
    c = c_ref[...]
    o_ref[...] = _dot(c * jax.nn.sigmoid(c), w_ref[...]) + b_ref[...]


def _adaln(c8, w_ada, b_ada):
    n = w_ada.shape[-1]
    return pl.pallas_call(
        _adaln_kernel,
        grid=(n // ADA_TN,),
        in_specs=[
            pl.BlockSpec((8, D_MODEL), lambda j: (0, 0)),
            pl.BlockSpec((None, D_MODEL, ADA_TN), lambda j: (0, 0, j)),
            pl.BlockSpec((1, ADA_TN), lambda j: (0, j)),
        ],
        out_specs=pl.BlockSpec((8, ADA_TN), lambda j: (0, j)),
        out_shape=jax.ShapeDtypeStruct((8, n), F32),
        compiler_params=_params(("arbitrary",)),
        name="adaln",
    )(c8, w_ada, b_ada)


def _in_proj_kernel(x_ref, mod_ref, g_ref, w_ref, qkg_ref, o_ref, h_ref):
    j = pl.program_id(1)

    @pl.when(j == 0)
    def _():
        xf = x_ref[...]
        ms = jnp.mean(xf * xf, axis=-1, keepdims=True)
        y = xf * lax.rsqrt(ms + EPS) * g_ref[...]
        h_ref[...] = (y * (1.0 + mod_ref[0, 1:2, :]) + mod_ref[0, 0:1, :]).astype(BF16)

    acc = _dot(h_ref[...], w_ref[...].astype(BF16))

    @pl.when(j < 2 * ATTN_WIDTH // PROJ_TN)
    def _():
        for hh in range(PROJ_TN // HEAD_DIM):
            sl = slice(hh * HEAD_DIM, (hh + 1) * HEAD_DIM)
            a = acc[:, sl]
            ms = jnp.mean(a * a, axis=-1, keepdims=True)
            o_ref[:, sl] = (a * lax.rsqrt(ms + EPS) * qkg_ref[:, sl]).astype(BF16)

    @pl.when(j >= 2 * ATTN_WIDTH // PROJ_TN)
    def _():
        o_ref[...] = acc.astype(BF16)


def _in_proj(x2, mod3, norm1_g, w_in, qk_gain):
    n_qk = 2 * ATTN_WIDTH // PROJ_TN
    rows_per_batch = SEQ // PROJ_TM
    return pl.pallas_call(
        _in_proj_kernel,
        grid=(N_TOK // PROJ_TM, IN_WIDTH // PROJ_TN),
        in_specs=[
            pl.BlockSpec((PROJ_TM, D_MODEL), lambda i, j: (i, 0)),
            pl.BlockSpec((1, 6, D_MODEL), lambda i, j: (i // rows_per_batch, 0, 0)),
            pl.BlockSpec((1, D_MODEL), lambda i, j: (0, 0)),
            pl.BlockSpec((None, D_MODEL, PROJ_TN), lambda i, j: (0, 0, j)),
            pl.BlockSpec((1, PROJ_TN), lambda i, j: (0, jnp.minimum(j, n_qk - 1))),
        ],
        out_specs=pl.BlockSpec((PROJ_TM, PROJ_TN), lambda i, j: (i, j)),
        out_shape=jax.ShapeDtypeStruct((N_TOK, IN_WIDTH), BF16),
        scratch_shapes=[pltpu.VMEM((PROJ_TM, D_MODEL), BF16)],
        compiler_params=_params(("arbitrary", "arbitrary")),
        name="in_proj",
    )(x2, mod3, norm1_g, w_in, qk_gain)


def _mixer_kernel(q_ref, k_ref, v_ref, u_ref, bg_ref, cg_ref, cw_ref, ag_ref, yg_ref,
                  a_ref, y_ref, s_ref):
    blk = MOBA_BLOCK
    k = k_ref[...]
    v = v_ref[...]

    km = jnp.mean(k.astype(F32).reshape(MOBA_NB, blk, HEAD_DIM), axis=1)
    km_hi = km.astype(BF16)
    km_lo = (km - km_hi.astype(F32)).astype(BF16)
    zpad = jnp.zeros((LANES - MOBA_NB, HEAD_DIM), BF16)
    km_hi = jnp.concatenate([km_hi, zpad], axis=0)
    km_lo = jnp.concatenate([km_lo, zpad], axis=0)

    lane = lax.broadcasted_iota(I32, (blk, LANES), 1)
    row = lax.broadcasted_iota(I32, (blk, blk), 0)
    col = lax.broadcasted_iota(I32, (blk, blk), 1)

    for i in range(MOBA_NB):
        qi = q_ref[i * blk:(i + 1) * blk, :]
        if i > 0:
            if i > MOBA_TOPK:
                g = _dot_nt(qi, km_hi) + _dot_nt(qi, km_lo)
                rank = jnp.zeros((blk, LANES), F32)
                for jp in range(i):
                    cj = g[:, jp:jp + 1]
                    tie = jnp.where(lane > jp, 1.0, 0.0)
                    rank = rank + jnp.where(cj > g, 1.0, jnp.where(cj == g, tie, 0.0))
                sel = jnp.where(rank < MOBA_TOPK, 1.0, 0.0)
            else:
                sel = jnp.ones((blk, LANES), F32)
            for j in range(i):
                s = _dot_nt(qi, k[j * blk:(j + 1) * blk])
                s_ref[:, j * blk:(j + 1) * blk] = jnp.where(sel[:, j:j + 1] > 0.5, s, MASKED)
        s = _dot_nt(qi, k[i * blk:(i + 1) * blk])
        s_ref[:, i * blk:(i + 1) * blk] = jnp.where(col <= row, s, MASKED)

        width = (i + 1) * blk
        sc = s_ref[:, :width]
        m = jnp.max(sc, axis=-1, keepdims=True)
        p = jnp.exp(sc - m)
        denom = jnp.sum(p, axis=-1, keepdims=True)
        o = _dot(p.astype(BF16), v[:width]) / denom
        ms = jnp.mean(o * o, axis=-1, keepdims=True)
        a_ref[i * blk:(i + 1) * blk, :] = (o * lax.rsqrt(ms + EPS) * ag_ref[...]).astype(BF16)

    z = cg_ref[...].astype(F32) * u_ref[...].astype(F32)
    t = lax.broadcasted_iota(I32, z.shape, 0)
    z1 = jnp.where(t >= 1, pltpu.roll(z, 1, 0), 0.0)
    z2 = jnp.where(t >= 2, pltpu.roll(z, 2, 0), 0.0)
    y = cw_ref[0:1, :] * z2 + cw_ref[1:2, :] * z1 + cw_ref[2:3, :] * z
    y = bg_ref[...].astype(F32) * y
    ms = jnp.mean(y * y, axis=-1, keepdims=True)
    y_ref[...] = (y * lax.rsqrt(ms + EPS) * yg_ref[...]).astype(BF16)


def _mixer(proj, conv_w, attn_out_g, conv_out_g):
    h = ATTN_HEADS

    def col(off):
        return pl.BlockSpec((SEQ, HEAD_DIM), lambda b, g: (b, off + g))

    out_spec = pl.BlockSpec((SEQ, HEAD_DIM), lambda b, g: (b, g))
    vec_spec = pl.BlockSpec((1, HEAD_DIM), lambda b, g: (0, g))
    return pl.pallas_call(
        _mixer_kernel,
        grid=(BATCH, h),
        in_specs=[col(0), col(h), col(2 * h), col(3 * h), col(4 * h), col(5 * h),
                  pl.BlockSpec((None, CONV_K, HEAD_DIM), lambda b, g: (0, 0, g)),
                  vec_spec, vec_spec],
        out_specs=[out_spec, out_spec],
        out_shape=[jax.ShapeDtypeStruct((N_TOK, ATTN_WIDTH), BF16),
                   jax.ShapeDtypeStruct((N_TOK, CONV_WIDTH), BF16)],
        scratch_shapes=[pltpu.VMEM((MOBA_BLOCK, SEQ), F32)],
        compiler_params=_params(("arbitrary", "arbitrary")),
        name="mixer",
    )(proj, proj, proj, proj, proj, proj, conv_w, attn_out_g, conv_out_g)


def _out_proj_kernel(a_ref, y_ref, wa_ref, wb_ref, x_ref, mod_ref, o_ref):
    mix = _dot(a_ref[...], wa_ref[...].astype(BF16)) + _dot(y_ref[...], wb_ref[...].astype(BF16))
    o_ref[...] = x_ref[...] + mod_ref[0, 2:3, :] * mix


def _out_proj(a, y, w_o, x2, mod3):
    rows_per_batch = SEQ // PROJ_TM
    return pl.pallas_call(
        _out_proj_kernel,
        grid=(N_TOK // PROJ_TM, D_MODEL // PROJ_TN),
        in_specs=[
            pl.BlockSpec((PROJ_TM, ATTN_WIDTH), lambda i, j: (i, 0)),
            pl.BlockSpec((PROJ_TM, CONV_WIDTH), lambda i, j: (i, 0)),
            pl.BlockSpec((None, ATTN_WIDTH, PROJ_TN), lambda i, j: (0, 0, j)),
            pl.BlockSpec((None, CONV_WIDTH, PROJ_TN), lambda i, j: (0, 1, j)),
            pl.BlockSpec((PROJ_TM, PROJ_TN), lambda i, j: (i, j)),
            pl.BlockSpec((1, 6, PROJ_TN), lambda i, j: (i // rows_per_batch, 0, j)),
        ],
        out_specs=pl.BlockSpec((PROJ_TM, PROJ_TN), lambda i, j: (i, j)),
        out_shape=jax.ShapeDtypeStruct((N_TOK, D_MODEL), F32),
        compiler_params=_params(("arbitrary", "arbitrary")),
        name="out_proj",
    )(a, y, w_o, w_o, x2, mod3)


def _norm2_modulate(xf, g_ref, mod_ref):
    ms = jnp.mean(xf * xf, axis=-1, keepdims=True)
    return xf * lax.rsqrt(ms + EPS) * g_ref[...] * (1.0 + mod_ref[0, 4:5, :]) + mod_ref[0, 3:4, :]


def _route_kernel(x_ref, g_ref, mod_ref, wrt_ref, bias_ref,
                  idx_ref, wt_ref, pos_ref, cnt_ref, carry_ref):
    tm = ROUTE_TM
    ne = N_EXPERTS

    @pl.when(pl.program_id(0) == 0)
    def _():
        carry_ref[...] = jnp.zeros_like(carry_ref)

    hb = _norm2_modulate(x_ref[...], g_ref, mod_ref).astype(BF16)

    scores = jax.nn.sigmoid(_dot_nt(wrt_ref[...].astype(BF16), hb))
    choice = scores + bias_ref[:, 0:1]
    ninf = -jnp.inf

    gi = lax.broadcasted_iota(I32, (GROUP_SIZE, tm), 0)
    rows = []
    for g in range(N_GROUPS):
        blk = choice[g * GROUP_SIZE:(g + 1) * GROUP_SIZE, :]
        m1 = jnp.max(blk, axis=0, keepdims=True)
        i1 = jnp.min(jnp.where(blk == m1, gi, GROUP_SIZE), axis=0, keepdims=True)
        m2 = jnp.max(jnp.where(gi == i1, ninf, blk), axis=0, keepdims=True)
        rows.append(m1 + m2)
    gsc = jnp.concatenate(rows, axis=0)

    gidx = lax.broadcasted_iota(I32, (N_GROUPS, tm), 0)
    rank = jnp.zeros((N_GROUPS, tm), F32)
    for gp in range(N_GROUPS):
        r = gsc[gp:gp + 1, :]
        tie = jnp.where(gidx > gp, 1.0, 0.0)
        rank = rank + jnp.where(r > gsc, 1.0, jnp.where(r == gsc, tie, 0.0))
    gsel = jnp.where(rank < TOPK_GROUPS, 1.0, 0.0)
    esel = jnp.concatenate(
        [jnp.broadcast_to(gsel[g:g + 1, :], (GROUP_SIZE, tm)) for g in range(N_GROUPS)], axis=0)
    masked = jnp.where(esel > 0.5, choice, ninf)

    eidx = lax.broadcasted_iota(I32, (ne, tm), 0)
    idx_rows, w_rows = [], []
    onehot = jnp.zeros((ne, tm), F32)
    for _ in range(TOP_K):
        m = jnp.max(masked, axis=0, keepdims=True)
        sel = jnp.min(jnp.where(masked == m, eidx, ne), axis=0, keepdims=True)
        hit = eidx == sel
        idx_rows.append(sel)
        w_rows.append(jnp.sum(jnp.where(hit, scores, 0.0), axis=0, keepdims=True))
        masked = jnp.where(hit, ninf, masked)
        onehot = jnp.where(hit, 1.0, onehot)
    wsel = jnp.concatenate(w_rows, axis=0)
    idx_ref[...] = jnp.concatenate(idx_rows, axis=0)
    wt_ref[...] = wsel / jnp.sum(wsel, axis=0, keepdims=True) * ROUTED_SCALE

    ti = lax.broadcasted_iota(I32, (tm, tm), 0)
    tj = lax.broadcasted_iota(I32, (tm, tm), 1)
    upper = jnp.where(ti < tj, 1.0, 0.0).astype(BF16)
    before = _dot(onehot.astype(BF16), upper) + carry_ref[:, 0:1]
    pos_rows = [jnp.sum(jnp.where(eidx == idx_rows[kk], before, 0.0), axis=0, keepdims=True)
                for kk in range(TOP_K)]
    pos_ref[...] = jnp.concatenate(pos_rows, axis=0).astype(I32)
    carry_ref[...] = carry_ref[...] + jnp.sum(onehot, axis=1, keepdims=True)
    cnt_ref[...] = carry_ref[...].astype(I32)


def _route(x1, norm2_g, mod3, w_rt, bias_col):
    tm = ROUTE_TM
    rows_per_batch = SEQ // tm
    full = lambda shape: pl.BlockSpec(shape, lambda i: tuple(0 for _ in shape))
    tok_tile = pl.BlockSpec((TOP_K, tm), lambda i: (0, i))
    return pl.pallas_call(
        _route_kernel,
        grid=(N_TOK // tm,),
        in_specs=[
            pl.BlockSpec((tm, D_MODEL), lambda i: (i, 0)),
            full((1, D_MODEL)),
            pl.BlockSpec((1, 6, D_MODEL), lambda i: (i // rows_per_batch, 0, 0)),
            full((N_EXPERTS, D_MODEL)),
            full((N_EXPERTS, LANES)),
        ],
        out_specs=[tok_tile, tok_tile, tok_tile, full((N_EXPERTS, LANES))],
        out_shape=[
            jax.ShapeDtypeStruct((TOP_K, N_TOK), I32),
            jax.ShapeDtypeStruct((TOP_K, N_TOK), F32),
            jax.ShapeDtypeStruct((TOP_K, N_TOK), I32),
            jax.ShapeDtypeStruct((N_EXPERTS, LANES), I32),
        ],
        scratch_shapes=[pltpu.VMEM((N_EXPERTS, LANES), F32)],
        compiler_params=_params(("arbitrary",)),
        name="route",
    )(x1, norm2_g, mod3, w_rt, bias_col)


def _dispatch_row_copy(hf_ref, row, xs_hbm, slot, sem):
    return pltpu.make_async_copy(hf_ref.at[pl.ds(row, 1), :], xs_hbm.at[pl.ds(slot, 1), :], sem)


def _shared_dispatch_kernel(start_ref, idx_ref, pos_ref, x_ref, g_ref, mod_ref, wsg_ref, wsu_ref, wsd_ref,
                            xs_hbm, base_ref, slot_ref, hf_ref, wsg_bf, wsu_bf, wsd_bf, sem):
    tm = DISP_TM

    @pl.when(pl.program_id(0) == 0)
    def _():
        wsg_bf[...] = wsg_ref[...].astype(BF16)
        wsu_bf[...] = wsu_ref[...].astype(BF16)
        wsd_bf[...] = wsd_ref[...].astype(BF16)

    xf = x_ref[...]
    hf = _norm2_modulate(xf, g_ref, mod_ref)
    hf_ref[...] = hf

    def issue(t, carry):
        for kk in range(TOP_K):
            slot = start_ref[idx_ref[kk, t]] + pos_ref[kk, t]
            slot_ref[kk, t] = slot
            _dispatch_row_copy(hf_ref, t, xs_hbm, slot, sem).start()
        return carry

    lax.fori_loop(0, tm, issue, 0)

    hb = hf.astype(BF16)
    hg = _dot(hb, wsg_bf[...])
    hu = _dot(hb, wsu_bf[...])
    hid = (hg * jax.nn.sigmoid(hg)) * hu
    shared = _dot(hid.astype(BF16), wsd_bf[...])
    base_ref[...] = xf + mod_ref[0, 5:6, :] * shared

    def drain(t, carry):
        for _ in range(TOP_K):
            _dispatch_row_copy(hf_ref, 0, xs_hbm, 0, sem).wait()
        return carry

    lax.fori_loop(0, tm, drain, 0)


def _shared_dispatch(grp_start, idx_t, pos_t, x1, norm2_g, mod3, ws_gate, ws_up, ws_down):
    tm = DISP_TM
    rows_per_batch = SEQ // tm
    tok_tile = pl.BlockSpec((TOP_K, tm), lambda i, st: (0, i), memory_space=pltpu.SMEM)
    row_tile = pl.BlockSpec((tm, D_MODEL), lambda i, st: (i, 0))
    grid_spec = pltpu.PrefetchScalarGridSpec(
        num_scalar_prefetch=1,
        grid=(N_TOK // tm,),
        in_specs=[
            tok_tile, tok_tile, row_tile,
            pl.BlockSpec((1, D_MODEL), lambda i, st: (0, 0)),
            pl.BlockSpec((1, 6, D_MODEL), lambda i, st: (i // rows_per_batch, 0, 0)),
            pl.BlockSpec((None, D_MODEL, SHARED_DIM), lambda i, st: (0, 0, 0)),
            pl.BlockSpec((None, D_MODEL, SHARED_DIM), lambda i, st: (0, 0, 0)),
            pl.BlockSpec((None, SHARED_DIM, D_MODEL), lambda i, st: (0, 0, 0)),
        ],
        out_specs=[pl.BlockSpec(memory_space=pl.ANY), row_tile, tok_tile],
        scratch_shapes=[
            pltpu.VMEM((tm, D_MODEL), F32),
            pltpu.VMEM((D_MODEL, SHARED_DIM), BF16),
            pltpu.VMEM((D_MODEL, SHARED_DIM), BF16),
            pltpu.VMEM((SHARED_DIM, D_MODEL), BF16),
            pltpu.SemaphoreType.DMA(()),
        ],
    )
    return pl.pallas_call(
        _shared_dispatch_kernel,
        grid_spec=grid_spec,
        out_shape=[jax.ShapeDtypeStruct((N_SLOTS, D_MODEL), F32),
                   jax.ShapeDtypeStruct((N_TOK, D_MODEL), F32),
                   jax.ShapeDtypeStruct((TOP_K, N_TOK), I32)],
        compiler_params=_params(("arbitrary",)),
        name="shared_dispatch",
    )(grp_start, idx_t, pos_t, x1, norm2_g, mod3, ws_gate, ws_up, ws_down)


def _weight_copies(hbm_refs, expert, bufs, slot, sems):
    return [pltpu.make_async_copy(h.at[0, expert], b.at[slot], sems.at[slot, j])
            for j, (h, b) in enumerate(zip(hbm_refs, bufs))]


def _experts_kernel(blk_ref, exp_ref, lo_ref, hi_ref, first_ref, par_ref, nxt_ref, n_ref,
                    xs_ref, wg_hbm, wu_hbm, wd_hbm, o_ref,
                    wg_buf, wu_buf, wd_buf, wg_bf, wu_bf, wd_bf, sems):
    i = pl.program_id(0)
    hbm = (wg_hbm, wu_hbm, wd_hbm)
    bufs = (wg_buf, wu_buf, wd_buf)
    active = i < n_ref[0]
    expert = exp_ref[i]
    slot = par_ref[i]

    @pl.when(i == 0)
    def _():
        for cp in _weight_copies(hbm, expert, bufs, 0, sems):
            cp.start()

    @pl.when(jnp.logical_and(active, first_ref[i] == 1))
    def _():
        for cp in _weight_copies(hbm, expert, bufs, slot, sems):
            cp.wait()
        nxt = nxt_ref[i]

        @pl.when(nxt >= 0)
        def _():
            for cp in _weight_copies(hbm, nxt, bufs, 1 - slot, sems):
                cp.start()

        wg_bf[...] = wg_buf[slot].astype(BF16)
        wu_bf[...] = wu_buf[slot].astype(BF16)
        wd_bf[...] = wd_buf[slot].astype(BF16)

    @pl.when(active)
    def _():
        x = xs_ref[...].astype(BF16)
        hg = _dot(x, wg_bf[...])
        hu = _dot(x, wu_bf[...])
        hid = (hg * jax.nn.sigmoid(hg)) * hu
        y = _dot(hid.astype(BF16), wd_bf[...])
        r = lax.broadcasted_iota(I32, (MOE_ROWS, 1), 0)
        lo = lo_ref[i]
        mine = (r >= lo) & (r < hi_ref[i])

        @pl.when(lo == 0)
        def _():
            o_ref[...] = jnp.where(mine, y, 0.0)

        @pl.when(lo > 0)
        def _():
            o_ref[...] = jnp.where(mine, y, o_ref[...])


def _experts(items, xs, w_gate, w_up, w_down):
    rows = pl.BlockSpec((MOE_ROWS, D_MODEL), lambda i, blk, *_: (blk[i], 0))
    hbm = pl.BlockSpec(memory_space=pl.ANY)
    grid_spec = pltpu.PrefetchScalarGridSpec(
        num_scalar_prefetch=len(items),
        grid=(MAX_ITEMS,),
        in_specs=[rows, hbm, hbm, hbm],
        out_specs=rows,
        scratch_shapes=[
            pltpu.VMEM((2, D_MODEL, EXPERT_DIM), F32),
            pltpu.VMEM((2, D_MODEL, EXPERT_DIM), F32),
            pltpu.VMEM((2, EXPERT_DIM, D_MODEL), F32),
            pltpu.VMEM((D_MODEL, EXPERT_DIM), BF16),
            pltpu.VMEM((D_MODEL, EXPERT_DIM), BF16),
            pltpu.VMEM((EXPERT_DIM, D_MODEL), BF16),
            pltpu.SemaphoreType.DMA((2, 3)),
        ],
    )
    return pl.pallas_call(
        _experts_kernel,
        grid_spec=grid_spec,
        out_shape=jax.ShapeDtypeStruct((N_SLOTS, D_MODEL), F32),
        compiler_params=_params(("arbitrary",)),
        name="experts",
    )(*items, xs, w_gate, w_up, w_down)


def _combine_kernel(slot_ref, wt_ref, base_ref, mod_ref, ys_hbm, o_ref, buf_ref, sem):
    tm = COMB_TM

    def issue(t, carry):
        for kk in range(TOP_K):
            pltpu.make_async_copy(ys_hbm.at[pl.ds(slot_ref[kk, t], 1), :],
                                  buf_ref.at[kk, pl.ds(t, 1), :], sem).start()
        return carry

    lax.fori_loop(0, tm, issue, 0)

    def drain(t, carry):
        for _ in range(TOP_K):
            pltpu.make_async_copy(ys_hbm.at[pl.ds(0, 1), :], buf_ref.at[0, pl.ds(0, 1), :], sem).wait()
        return carry

    lax.fori_loop(0, tm, drain, 0)

    w_tok = wt_ref[...].T
    acc = w_tok[:, 0:1] * buf_ref[0]
    for kk in range(1, TOP_K):
        acc = acc + w_tok[:, kk:kk + 1] * buf_ref[kk]
    o_ref[...] = base_ref[...] + mod_ref[0, 5:6, :] * acc


def _combine(slot_t, w_t, base, mod3, ys):
    tm = COMB_TM
    rows_per_batch = SEQ // tm
    return pl.pallas_call(
        _combine_kernel,
        grid=(N_TOK // tm,),
        in_specs=[
            pl.BlockSpec((TOP_K, tm), lambda i: (0, i), memory_space=pltpu.SMEM),
            pl.BlockSpec((TOP_K, tm), lambda i: (0, i)),
            pl.BlockSpec((tm, D_MODEL), lambda i: (i, 0)),
            pl.BlockSpec((1, 6, D_MODEL), lambda i: (i // rows_per_batch, 0, 0)),
            pl.BlockSpec(memory_space=pl.ANY),
        ],
        out_specs=pl.BlockSpec((tm, D_MODEL), lambda i: (i, 0)),
        out_shape=jax.ShapeDtypeStruct((N_TOK, D_MODEL), F32),
        scratch_shapes=[pltpu.VMEM((TOP_K, tm, D_MODEL), F32), pltpu.SemaphoreType.DMA(())],
        compiler_params=_params(("arbitrary",)),
        name="combine",
    )(slot_t, w_t, base, mod3, ys)


def _work_items(counts):
    rows = MOE_ROWS
    grp_end = jnp.cumsum(counts)
    grp_start = grp_end - counts
    first_blk = grp_start // rows
    last_blk = (grp_end - 1) // rows
    n_blk = jnp.where(counts > 0, last_blk - first_blk + 1, 0)
    item_end = jnp.cumsum(n_blk)
    item_start = item_end - n_blk
    n_items = item_end[-1]

    ids = jnp.arange(N_EXPERTS, dtype=I32)
    nonempty = (n_blk > 0).astype(I32)
    ordinal = jnp.cumsum(nonempty) - nonempty
    later = lax.cummin(jnp.where(nonempty > 0, ids, N_EXPERTS), axis=0, reverse=True)
    nxt_e = jnp.concatenate([later[1:], jnp.full((1,), N_EXPERTS, I32)])
    nxt_e = jnp.where(nxt_e >= N_EXPERTS, -1, nxt_e)

    i = jnp.minimum(jnp.arange(MAX_ITEMS, dtype=I32), n_items - 1)
    own = (item_start[None, :] <= i[:, None]) & (i[:, None] < item_end[None, :])
    pick = lambda v: jnp.sum(jnp.where(own, v[None, :], 0), axis=1).astype(I32)
    blk = pick(first_blk - item_start) + i
    lo = jnp.maximum(pick(grp_start), blk * rows) - blk * rows
    hi = jnp.minimum(pick(grp_end), (blk + 1) * rows) - blk * rows
    first = (pick(item_start) == i).astype(I32)
    items = (blk, pick(ids), lo, hi, first, pick(ordinal) % 2, pick(nxt_e),
             n_items.astype(I32).reshape(1))
    return grp_start.astype(I32), items


def kernel(x, c, w_ada, b_ada, norm1_g, w_in, q_norm_g, k_norm_g, conv_w, attn_out_g, conv_out_g,
           w_o, norm2_g, w_router, router_bias, w_gate, w_up, w_down, ws_gate, ws_up, ws_down):
    b, s, d = x.shape
    x2 = x.reshape(b * s, d)

    c8 = jnp.pad(c, ((0, 8 - b), (0, 0)))
    mod3 = _adaln(c8, w_ada, b_ada)[:b].reshape(b, 6, d)

    qk_gain = jnp.concatenate([jnp.tile(q_norm_g[0] * HEAD_DIM ** -0.5, ATTN_HEADS),
                               jnp.tile(k_norm_g[0], ATTN_HEADS)]).reshape(1, 2 * ATTN_WIDTH)
    proj = _in_proj(x2, mod3, norm1_g, w_in, qk_gain)

    a, y = _mixer(proj, conv_w, attn_out_g, conv_out_g)
    x1 = _out_proj(a, y, w_o, x2, mod3)

    w_rt = w_router[0].T
    bias_col = jnp.broadcast_to(router_bias[0][:, None], (N_EXPERTS, LANES))
    idx_t, w_t, pos_t, cnt = _route(x1, norm2_g, mod3, w_rt, bias_col)

    grp_start, items = _work_items(cnt[:, 0])
    xs, base, slot_t = _shared_dispatch(grp_start, idx_t, pos_t, x1, norm2_g, mod3, ws_gate, ws_up, ws_down)
    ys = _experts(items, xs, w_gate, w_up, w_down)
    out = _combine(slot_t, w_t, base, mod3, ys)
    return out.reshape(b, s, d)
```

```python
import functools

import jax
import jax.numpy as jnp
from jax import lax
from jax.experimental import pallas as pl
from jax.experimental.pallas import tpu as pltpu

F32 = jnp.float32
BF16 = jnp.bfloat16
I32 = jnp.int32

D_MODEL = 2048
BATCH = 4
SEQ = 2048
N_TOK = BATCH * SEQ
HEAD_DIM = 128
ATTN_WIDTH = 1024
ATTN_HEADS = 8
CONV_WIDTH = 1024
CONV_GROUPS = 8
IN_WIDTH = 6144
CONV_K = 3
MOBA_BLOCK = 256
MOBA_NB = SEQ // MOBA_BLOCK
MOBA_TOPK = 3
N_EXPERTS = 256
TOP_K = 8
N_GROUPS = 8
GROUP_SIZE = N_EXPERTS // N_GROUPS
TOPK_GROUPS = 4
EXPERT_DIM = 512
SHARED_DIM = 512
ROUTED_SCALE = 2.5
EPS = 1e-6

LANES = 128
SUBLANES = 8
MXU_COLS = 256
MASKED = -1e30
VMEM_LIMIT = 56 * 1024 * 1024

ADA_TN = 1024
PROJ_TM = 1024
PROJ_TN = 512
ROUTE_TM = 256
DISP_TM = 256
MOE_ROWS = 128
N_SLOTS = N_TOK * TOP_K
N_ROW_BLOCKS = N_SLOTS // MOE_ROWS
MAX_ITEMS = N_ROW_BLOCKS + N_EXPERTS
COMB_TM = 128


def _dot(a, b):
    return jnp.dot(a, b, preferred_element_type=F32)


def _dot_nt(a, b):
    return lax.dot_general(a, b, (((1,), (1,)), ((), ())), preferred_element_type=F32)


def _params(sem):
    return pltpu.CompilerParams(dimension_semantics=sem, vmem_limit_bytes=VMEM_LIMIT)


def _adaln_kernel(c_ref, w_ref, b_ref, o_ref):
    c = c_ref[...]
    o_ref[...] = _dot(c * jax.nn.sigmoid(c), w_ref[...]) + b_ref[...]


def _adaln(c8, w_ada, b_ada):
    n = w_ada.shape[-1]
    return pl.pallas_call(
        _adaln_kernel,
        grid=(n // ADA_TN,),
        in_specs=[
            pl.BlockSpec((8, D_MODEL), lambda j: (0, 0)),
            pl.BlockSpec((None, D_MODEL, ADA_TN), lambda j: (0, 0, j)),
            pl.BlockSpec((1, ADA_TN), lambda j: (0, j)),
        ],
        out_specs=pl.BlockSpec((8, ADA_TN), lambda j: (0, j)),
        out_shape=jax.ShapeDtypeStruct((8, n), F32),
        compiler_params=_params(("arbitrary",)),
        name="adaln",
    )(c8, w_ada, b_ada)


def _in_proj_kernel(x_ref, mod_ref, g_ref, w_ref, qkg_ref, o_ref, h_ref):
    j = pl.program_id(1)

    @pl.when(j == 0)
    def _():
        xf = x_ref[...]
        ms = jnp.mean(xf * xf, axis=-1, keepdims=True)
        y = xf * lax.rsqrt(ms + EPS) * g_ref[...]
        h_ref[...] = (y * (1.0 + mod_ref[0, 1:2, :]) + mod_ref[0, 0:1, :]).astype(BF16)

    @pl.when(j < 2 * ATTN_WIDTH // PROJ_TN)
    def _():
        for pair in range(PROJ_TN // MXU_COLS):
            cols = slice(pair * MXU_COLS, (pair + 1) * MXU_COLS)
            acc = _dot(h_ref[...], w_ref[:, cols].astype(BF16))
            for hh in range(MXU_COLS // HEAD_DIM):
                a = acc[:, hh * HEAD_DIM:(hh + 1) * HEAD_DIM]
                sl = slice(pair * MXU_COLS + hh * HEAD_DIM, pair * MXU_COLS + (hh + 1) * HEAD_DIM)
                ms = jnp.mean(a * a, axis=-1, keepdims=True)
                o_ref[:, sl] = (a * lax.rsqrt(ms + EPS) * qkg_ref[:, sl]).astype(BF16)

    @pl.when(j >= 2 * ATTN_WIDTH // PROJ_TN)
    def _():
        o_ref[...] = _dot(h_ref[...], w_ref[...].astype(BF16)).astype(BF16)


def _in_proj(x2, mod3, norm1_g, w_in, qk_gain):
    n_qk = 2 * ATTN_WIDTH // PROJ_TN
    rows_per_batch = SEQ // PROJ_TM
    return pl.pallas_call(
        _in_proj_kernel,
        grid=(N_TOK // PROJ_TM, IN_WIDTH // PROJ_TN),
        in_specs=[
            pl.BlockSpec((PROJ_TM, D_MODEL), lambda i, j: (i, 0)),
            pl.BlockSpec((1, 6, D_MODEL), lambda i, j: (i // rows_per_batch, 0, 0)),
            pl.BlockSpec((1, D_MODEL), lambda i, j: (0, 0)),
            pl.BlockSpec((None, D_MODEL, PROJ_TN), lambda i, j: (0, 0, j)),
            pl.BlockSpec((1, PROJ_TN), lambda i, j: (0, jnp.minimum(j, n_qk - 1))),
        ],
        out_specs=pl.BlockSpec((PROJ_TM, PROJ_TN), lambda i, j: (i, j)),
        out_shape=jax.ShapeDtypeStruct((N_TOK, IN_WIDTH), BF16),
        scratch_shapes=[pltpu.VMEM((PROJ_TM, D_MODEL), BF16)],
        compiler_params=_params(("arbitrary", "arbitrary")),
        name="in_proj",
    )(x2, mod3, norm1_g, w_in, qk_gain)


def _mixer_kernel(q_ref, k_ref, v_ref, u_ref, bg_ref, cg_ref, cw_ref, ag_ref, yg_ref,
                  a_ref, y_ref, s_ref):
    blk = MOBA_BLOCK
    k = k_ref[...]
    v = v_ref[...]

    km = jnp.mean(k.astype(F32).reshape(MOBA_NB, blk, HEAD_DIM), axis=1)
    km_hi = km.astype(BF16)
    km_lo = (km - km_hi.astype(F32)).astype(BF16)
    zpad = jnp.zeros((LANES - MOBA_NB, HEAD_DIM), BF16)
    km_hi = jnp.concatenate([km_hi, zpad], axis=0)
    km_lo = jnp.concatenate([km_lo, zpad], axis=0)

    lane = lax.broadcasted_iota(I32, (blk, LANES), 1)
    row = lax.broadcasted_iota(I32, (blk, blk), 0)
    col = lax.broadcasted_iota(I32, (blk, blk), 1)

    for i in range(MOBA_NB):
        qi = q_ref[i * blk:(i + 1) * blk, :]
        if i > 0:
            if i > MOBA_TOPK:
                g = _dot_nt(qi, km_hi) + _dot_nt(qi, km_lo)
                rank = jnp.zeros((blk, LANES), F32)
                for jp in range(i):
                    cj = g[:, jp:jp + 1]
                    tie = jnp.where(lane > jp, 1.0, 0.0)
                    rank = rank + jnp.where(cj > g, 1.0, jnp.where(cj == g, tie, 0.0))
                sel = jnp.where(rank < MOBA_TOPK, 1.0, 0.0)
            else:
                sel = jnp.ones((blk, LANES), F32)
            for j in range(i):
                s = _dot_nt(qi, k[j * blk:(j + 1) * blk])
                s_ref[:, j * blk:(j + 1) * blk] = jnp.where(sel[:, j:j + 1] > 0.5, s, MASKED)
        s = _dot_nt(qi, k[i * blk:(i + 1) * blk])
        s_ref[:, i * blk:(i + 1) * blk] = jnp.where(col <= row, s, MASKED)

        width = (i + 1) * blk
        sc = s_ref[:, :width]
        m = jnp.max(sc, axis=-1, keepdims=True)
        p = jnp.exp(sc - m)
        denom = jnp.sum(p, axis=-1, keepdims=True)
        o = _dot(p.astype(BF16), v[:width]) / denom
        ms = jnp.mean(o * o, axis=-1, keepdims=True)
        a_ref[i * blk:(i + 1) * blk, :] = (o * lax.rsqrt(ms + EPS) * ag_ref[...]).astype(BF16)

    z = cg_ref[...].astype(F32) * u_ref[...].astype(F32)
    t = lax.broadcasted_iota(I32, z.shape, 0)
    z1 = jnp.where(t >= 1, pltpu.roll(z, 1, 0), 0.0)
    z2 = jnp.where(t >= 2, pltpu.roll(z, 2, 0), 0.0)
    y = cw_ref[0:1, :] * z2 + cw_ref[1:2, :] * z1 + cw_ref[2:3, :] * z
    y = bg_ref[...].astype(F32) * y
    ms = jnp.mean(y * y, axis=-1, keepdims=True)
    y_ref[...] = (y * lax.rsqrt(ms + EPS) * yg_ref[...]).astype(BF16)


def _mixer(proj, conv_w, attn_out_g, conv_out_g):
    h = ATTN_HEADS

    def col(off):
        return pl.BlockSpec((SEQ, HEAD_DIM), lambda b, g: (b, off + g))

    out_spec = pl.BlockSpec((SEQ, HEAD_DIM), lambda b, g: (b, g))
    vec_spec = pl.BlockSpec((1, HEAD_DIM), lambda b, g: (0, g))
    return pl.pallas_call(
        _mixer_kernel,
        grid=(BATCH, h),
        in_specs=[col(0), col(h), col(2 * h), col(3 * h), col(4 * h), col(5 * h),
                  pl.BlockSpec((None, CONV_K, HEAD_DIM), lambda b, g: (0, 0, g)),
                  vec_spec, vec_spec],
        out_specs=[out_spec, out_spec],
        out_shape=[jax.ShapeDtypeStruct((N_TOK, ATTN_WIDTH), BF16),
                   jax.ShapeDtypeStruct((N_TOK, CONV_WIDTH), BF16)],
        scratch_shapes=[pltpu.VMEM((MOBA_BLOCK, SEQ), F32)],
        compiler_params=_params(("arbitrary", "arbitrary")),
        name="mixer",
    )(proj, proj, proj, proj, proj, proj, conv_w, attn_out_g, conv_out_g)


def _out_proj_kernel(a_ref, y_ref, wa_ref, wb_ref, x_ref, mod_ref, o_ref):
    mix = _dot(a_ref[...], wa_ref[...].astype(BF16)) + _dot(y_ref[...], wb_ref[...].astype(BF16))
    o_ref[...] = x_ref[...] + mod_ref[0, 2:3, :] * mix


def _out_proj(a, y, w_o, x2, mod3):
    rows_per_batch = SEQ // PROJ_TM
    return pl.pallas_call(
        _out_proj_kernel,
        grid=(N_TOK // PROJ_TM, D_MODEL // PROJ_TN),
        in_specs=[
            pl.BlockSpec((PROJ_TM, ATTN_WIDTH), lambda i, j: (i, 0)),
            pl.BlockSpec((PROJ_TM, CONV_WIDTH), lambda i, j: (i, 0)),
            pl.BlockSpec((None, ATTN_WIDTH, PROJ_TN), lambda i, j: (0, 0, j)),
            pl.BlockSpec((None, CONV_WIDTH, PROJ_TN), lambda i, j: (0, 1, j)),
            pl.BlockSpec((PROJ_TM, PROJ_TN), lambda i, j: (i, j)),
            pl.BlockSpec((1, 6, PROJ_TN), lambda i, j: (i // rows_per_batch, 0, j)),
        ],
        out_specs=pl.BlockSpec((PROJ_TM, PROJ_TN), lambda i, j: (i, j)),
        out_shape=jax.ShapeDtypeStruct((N_TOK, D_MODEL), F32),
        compiler_params=_params(("arbitrary", "arbitrary")),
        name="out_proj",
    )(a, y, w_o, w_o, x2, mod3)


def _norm2_modulate(xf, g_ref, mod_ref):
    ms = jnp.mean(xf * xf, axis=-1, keepdims=True)
    return xf * lax.rsqrt(ms + EPS) * g_ref[...] * (1.0 + mod_ref[0, 4:5, :]) + mod_ref[0, 3:4, :]


def _route_kernel(x_ref, g_ref, mod_ref, wrt_ref, bias_ref,
                  idx_ref, wt_ref, pos_ref, cnt_ref, carry_ref):
    tm = ROUTE_TM
    ne = N_EXPERTS

    @pl.when(pl.program_id(0) == 0)
    def _():
        carry_ref[...] = jnp.zeros_like(carry_ref)

    hb = _norm2_modulate(x_ref[...], g_ref, mod_ref).astype(BF16)

    scores = jax.nn.sigmoid(_dot_nt(wrt_ref[...].astype(BF16), hb))
    choice = scores + bias_ref[:, 0:1]
    ninf = -jnp.inf

    gi = lax.broadcasted_iota(I32, (GROUP_SIZE, tm), 0)
    rows = []
    for g in range(N_GROUPS):
        blk = choice[g * GROUP_SIZE:(g + 1) * GROUP_SIZE, :]
        m1 = jnp.max(blk, axis=0, keepdims=True)
        i1 = jnp.min(jnp.where(blk == m1, gi, GROUP_SIZE), axis=0, keepdims=True)
        m2 = jnp.max(jnp.where(gi == i1, ninf, blk), axis=0, keepdims=True)
        rows.append(m1 + m2)
    gsc = jnp.concatenate(rows, axis=0)

    gidx = lax.broadcasted_iota(I32, (N_GROUPS, tm), 0)
    rank = jnp.zeros((N_GROUPS, tm), F32)
    for gp in range(N_GROUPS):
        r = gsc[gp:gp + 1, :]
        tie = jnp.where(gidx > gp, 1.0, 0.0)
        rank = rank + jnp.where(r > gsc, 1.0, jnp.where(r == gsc, tie, 0.0))
    gsel = jnp.where(rank < TOPK_GROUPS, 1.0, 0.0)
    esel = jnp.concatenate(
        [jnp.broadcast_to(gsel[g:g + 1, :], (GROUP_SIZE, tm)) for g in range(N_GROUPS)], axis=0)
    masked = jnp.where(esel > 0.5, choice, ninf)

    eidx = lax.broadcasted_iota(I32, (ne, tm), 0)
    idx_rows, w_rows = [], []
    onehot = jnp.zeros((ne, tm), F32)
    for _ in range(TOP_K):
        m = jnp.max(masked, axis=0, keepdims=True)
        sel = jnp.min(jnp.where(masked == m, eidx, ne), axis=0, keepdims=True)
        hit = eidx == sel
        idx_rows.append(sel)
        w_rows.append(jnp.sum(jnp.where(hit, scores, 0.0), axis=0, keepdims=True))
        masked = jnp.where(hit, ninf, masked)
        onehot = jnp.where(hit, 1.0, onehot)
    wsel = jnp.concatenate(w_rows, axis=0)
    idx_ref[...] = jnp.concatenate(idx_rows, axis=0)
    wt_ref[...] = wsel / jnp.sum(wsel, axis=0, keepdims=True) * ROUTED_SCALE

    ti = lax.broadcasted_iota(I32, (tm, tm), 0)
    tj = lax.broadcasted_iota(I32, (tm, tm), 1)
    upper = jnp.where(ti < tj, 1.0, 0.0).astype(BF16)
    before = _dot(onehot.astype(BF16), upper) + carry_ref[:, 0:1]
    pos_rows = [jnp.sum(jnp.where(eidx == idx_rows[kk], before, 0.0), axis=0, keepdims=True)
                for kk in range(TOP_K)]
    pos_ref[...] = jnp.concatenate(pos_rows, axis=0).astype(I32)
    carry_ref[...] = carry_ref[...] + jnp.sum(onehot, axis=1, keepdims=True)
    cnt_ref[...] = carry_ref[...].astype(I32)


def _route(x1, norm2_g, mod3, w_rt, bias_col):
    tm = ROUTE_TM
    rows_per_batch = SEQ // tm
    full = lambda shape: pl.BlockSpec(shape, lambda i: tuple(0 for _ in shape))
    tok_tile = pl.BlockSpec((TOP_K, tm), lambda i: (0, i))
    return pl.pallas_call(
        _route_kernel,
        grid=(N_TOK // tm,),
        in_specs=[
            pl.BlockSpec((tm, D_MODEL), lambda i: (i, 0)),
            full((1, D_MODEL)),
            pl.BlockSpec((1, 6, D_MODEL), lambda i: (i // rows_per_batch, 0, 0)),
            full((N_EXPERTS, D_MODEL)),
            full((N_EXPERTS, LANES)),
        ],
        out_specs=[tok_tile, tok_tile, tok_tile, full((N_EXPERTS, LANES))],
        out_shape=[
            jax.ShapeDtypeStruct((TOP_K, N_TOK), I32),
            jax.ShapeDtypeStruct((TOP_K, N_TOK), F32),
            jax.ShapeDtypeStruct((TOP_K, N_TOK), I32),
            jax.ShapeDtypeStruct((N_EXPERTS, LANES), I32),
        ],
        scratch_shapes=[pltpu.VMEM((N_EXPERTS, LANES), F32)],
        compiler_params=_params(("arbitrary",)),
        name="route",
    )(x1, norm2_g, mod3, w_rt, bias_col)


def _dispatch_row_copy(hf_ref, row, xs_hbm, slot, sem):
    return pltpu.make_async_copy(hf_ref.at[pl.ds(row, 1), :], xs_hbm.at[pl.ds(slot, 1), :], sem)


def _shared_dispatch_kernel(start_ref, idx_ref, pos_ref, x_ref, g_ref, mod_ref, wsg_ref, wsu_ref, wsd_ref,
                            xs_hbm, base_ref, slot_ref, hf_ref, wsg_bf, wsu_bf, wsd_bf, sem):
    tm = DISP_TM

    @pl.when(pl.program_id(0) == 0)
    def _():
        wsg_bf[...] = wsg_ref[...].astype(BF16)
        wsu_bf[...] = wsu_ref[...].astype(BF16)
        wsd_bf[...] = wsd_ref[...].astype(BF16)

    xf = x_ref[...]
    hf = _norm2_modulate(xf, g_ref, mod_ref)
    hf_ref[...] = hf

    def issue(t8, carry):
        t0 = pl.multiple_of(t8 * SUBLANES, SUBLANES)
        for r in range(SUBLANES):
            for kk in range(TOP_K):
                slot = start_ref[idx_ref[kk, t0 + r]] + pos_ref[kk, t0 + r]
                slot_ref[kk, t0 + r] = slot
                _dispatch_row_copy(hf_ref, t0 + r, xs_hbm, slot, sem).start()
        return carry

    lax.fori_loop(0, tm // SUBLANES, issue, 0)

    hb = hf.astype(BF16)
    hg = _dot(hb, wsg_bf[...])
    hu = _dot(hb, wsu_bf[...])
    hid = (hg * jax.nn.sigmoid(hg)) * hu
    shared = _dot(hid.astype(BF16), wsd_bf[...])
    base_ref[...] = xf + mod_ref[0, 5:6, :] * shared

    def drain(t8, carry):
        for _ in range(SUBLANES * TOP_K):
            _dispatch_row_copy(hf_ref, 0, xs_hbm, 0, sem).wait()
        return carry

    lax.fori_loop(0, tm // SUBLANES, drain, 0)


def _shared_dispatch(grp_start, idx_t, pos_t, x1, norm2_g, mod3, ws_gate, ws_up, ws_down):
    tm = DISP_TM
    rows_per_batch = SEQ // tm
    tok_tile = pl.BlockSpec((TOP_K, tm), lambda i, st: (0, i), memory_space=pltpu.SMEM)
    row_tile = pl.BlockSpec((tm, D_MODEL), lambda i, st: (i, 0))
    grid_spec = pltpu.PrefetchScalarGridSpec(
        num_scalar_prefetch=1,
        grid=(N_TOK // tm,),
        in_specs=[
            tok_tile, tok_tile, row_tile,
            pl.BlockSpec((1, D_MODEL), lambda i, st: (0, 0)),
            pl.BlockSpec((1, 6, D_MODEL), lambda i, st: (i // rows_per_batch, 0, 0)),
            pl.BlockSpec((None, D_MODEL, SHARED_DIM), lambda i, st: (0, 0, 0)),
            pl.BlockSpec((None, D_MODEL, SHARED_DIM), lambda i, st: (0, 0, 0)),
            pl.BlockSpec((None, SHARED_DIM, D_MODEL), lambda i, st: (0, 0, 0)),
        ],
        out_specs=[pl.BlockSpec(memory_space=pl.ANY), row_tile, tok_tile],
        scratch_shapes=[
            pltpu.VMEM((tm, D_MODEL), F32),
            pltpu.VMEM((D_MODEL, SHARED_DIM), BF16),
            pltpu.VMEM((D_MODEL, SHARED_DIM), BF16),
            pltpu.VMEM((SHARED_DIM, D_MODEL), BF16),
            pltpu.SemaphoreType.DMA(()),
        ],
    )
    return pl.pallas_call(
        _shared_dispatch_kernel,
        grid_spec=grid_spec,
        out_shape=[jax.ShapeDtypeStruct((N_SLOTS, D_MODEL), F32),
                   jax.ShapeDtypeStruct((N_TOK, D_MODEL), F32),
                   jax.ShapeDtypeStruct((TOP_K, N_TOK), I32)],
        compiler_params=_params(("arbitrary",)),
        name="shared_dispatch",
    )(grp_start, idx_t, pos_t, x1, norm2_g, mod3, ws_gate, ws_up, ws_down)


def _weight_copies(hbm_refs, expert, bufs, slot, sems):
    return [pltpu.make_async_copy(h.at[0, expert], b.at[slot], sems.at[slot, j])
            for j, (h, b) in enumerate(zip(hbm_refs, bufs))]


def _experts_kernel(blk_ref, exp_ref, lo_ref, hi_ref, first_ref, par_ref, nxt_ref, n_ref,
                    xs_ref, wg_hbm, wu_hbm, wd_hbm, o_ref,
                    wg_buf, wu_buf, wd_buf, wg_bf, wu_bf, wd_bf, sems):
    i = pl.program_id(0)
    hbm = (wg_hbm, wu_hbm, wd_hbm)
    bufs = (wg_buf, wu_buf, wd_buf)
    active = i < n_ref[0]
    expert = exp_ref[i]
    slot = par_ref[i]

    @pl.when(i == 0)
    def _():
        for cp in _weight_copies(hbm, expert, bufs, 0, sems):
            cp.start()

    @pl.when(jnp.logical_and(active, first_ref[i] == 1))
    def _():
        for cp in _weight_copies(hbm, expert, bufs, slot, sems):
            cp.wait()
        nxt = nxt_ref[i]

        @pl.when(nxt >= 0)
        def _():
            for cp in _weight_copies(hbm, nxt, bufs, 1 - slot, sems):
                cp.start(priority=1)

        wg_bf[...] = wg_buf[slot].astype(BF16)
        wu_bf[...] = wu_buf[slot].astype(BF16)
        wd_bf[...] = wd_buf[slot].astype(BF16)

    @pl.when(active)
    def _():
        x = xs_ref[...].astype(BF16)
        hg = _dot(x, wg_bf[...])
        hu = _dot(x, wu_bf[...])
        hid = (hg * jax.nn.sigmoid(hg)) * hu
        y = _dot(hid.astype(BF16), wd_bf[...])
        r = lax.broadcasted_iota(I32, (MOE_ROWS, 1), 0)
        lo = lo_ref[i]
        mine = (r >= lo) & (r < hi_ref[i])

        @pl.when(lo == 0)
        def _():
            o_ref[...] = jnp.where(mine, y, 0.0)

        @pl.when(lo > 0)
        def _():
            o_ref[...] = jnp.where(mine, y, o_ref[...])


def _experts(items, xs, w_gate, w_up, w_down):
    rows = pl.BlockSpec((MOE_ROWS, D_MODEL), lambda i, blk, *_: (blk[i], 0))
    hbm = pl.BlockSpec(memory_space=pl.ANY)
    grid_spec = pltpu.PrefetchScalarGridSpec(
        num_scalar_prefetch=len(items),
        grid=(MAX_ITEMS,),
        in_specs=[rows, hbm, hbm, hbm],
        out_specs=rows,
        scratch_shapes=[
            pltpu.VMEM((2, D_MODEL, EXPERT_DIM), F32),
            pltpu.VMEM((2, D_MODEL, EXPERT_DIM), F32),
            pltpu.VMEM((2, EXPERT_DIM, D_MODEL), F32),
            pltpu.VMEM((D_MODEL, EXPERT_DIM), BF16),
            pltpu.VMEM((D_MODEL, EXPERT_DIM), BF16),
            pltpu.VMEM((EXPERT_DIM, D_MODEL), BF16),
            pltpu.SemaphoreType.DMA((2, 3)),
        ],
    )
    return pl.pallas_call(
        _experts_kernel,
        grid_spec=grid_spec,
        out_shape=jax.ShapeDtypeStruct((N_SLOTS, D_MODEL), F32),
        compiler_params=_params(("arbitrary",)),
        name="experts",
    )(*items, xs, w_gate, w_up, w_down)


def _combine_kernel(slot_ref, wt_ref, base_ref, mod_ref, ys_hbm, o_ref, buf_ref, sem):
    tm = COMB_TM

    def row_copy(slot, kk, t):
        return pltpu.make_async_copy(ys_hbm.at[pl.ds(slot, 1), :], buf_ref.at[kk, pl.ds(t, 1), :], sem)

    def issue(t8, carry):
        t0 = pl.multiple_of(t8 * SUBLANES, SUBLANES)
        for r in range(SUBLANES):
            for kk in range(TOP_K):
                row_copy(slot_ref[kk, t0 + r], kk, t0 + r).start()
        return carry

    lax.fori_loop(0, tm // SUBLANES, issue, 0)

    def drain(t8, carry):
        for _ in range(SUBLANES * TOP_K):
            row_copy(0, 0, 0).wait()
        return carry

    lax.fori_loop(0, tm // SUBLANES, drain, 0)

    w_tok = wt_ref[...].T
    acc = w_tok[:, 0:1] * buf_ref[0]
    for kk in range(1, TOP_K):
        acc = acc + w_tok[:, kk:kk + 1] * buf_ref[kk]
    o_ref[...] = base_ref[...] + mod_ref[0, 5:6, :] * acc


def _combine(slot_t, w_t, base, mod3, ys):
    tm = COMB_TM
    rows_per_batch = SEQ // tm
    return pl.pallas_call(
        _combine_kernel,
        grid=(N_TOK // tm,),
        in_specs=[
            pl.BlockSpec((TOP_K, tm), lambda i: (0, i), memory_space=pltpu.SMEM),
            pl.BlockSpec((TOP_K, tm), lambda i: (0, i)),
            pl.BlockSpec((tm, D_MODEL), lambda i: (i, 0)),
            pl.BlockSpec((1, 6, D_MODEL), lambda i: (i // rows_per_batch, 0, 0)),
            pl.BlockSpec(memory_space=pl.ANY),
        ],
        out_specs=pl.BlockSpec((tm, D_MODEL), lambda i: (i, 0)),
        out_shape=jax.ShapeDtypeStruct((N_TOK, D_MODEL), F32),
        scratch_shapes=[pltpu.VMEM((TOP_K, tm, D_MODEL), F32), pltpu.SemaphoreType.DMA(())],
        compiler_params=_params(("arbitrary",)),
        name="combine",
    )(slot_t, w_t, base, mod3, ys)


def _work_items(counts):
    rows = MOE_ROWS
    grp_end = jnp.cumsum(counts)
    grp_start = grp_end - counts
    first_blk = grp_start // rows
    last_blk = (grp_end - 1) // rows
    n_blk = jnp.where(counts > 0, last_blk - first_blk + 1, 0)
    item_end = jnp.cumsum(n_blk)
    item_start = item_end - n_blk
    n_items = item_end[-1]

    ids = jnp.arange(N_EXPERTS, dtype=I32)
    nonempty = (n_blk > 0).astype(I32)
    ordinal = jnp.cumsum(nonempty) - nonempty
    later = lax.cummin(jnp.where(nonempty > 0, ids, N_EXPERTS), axis=0, reverse=True)
    nxt_e = jnp.concatenate([later[1:], jnp.full((1,), N_EXPERTS, I32)])
    nxt_e = jnp.where(nxt_e >= N_EXPERTS, -1, nxt_e)

    i = jnp.minimum(jnp.arange(MAX_ITEMS, dtype=I32), n_items - 1)
    own = (item_start[None, :] <= i[:, None]) & (i[:, None] < item_end[None, :])
    pick = lambda v: jnp.sum(jnp.where(own, v[None, :], 0), axis=1).astype(I32)
    blk = pick(first_blk - item_start) + i
    lo = jnp.maximum(pick(grp_start), blk * rows) - blk * rows
    hi = jnp.minimum(pick(grp_end), (blk + 1) * rows) - blk * rows
    first = (pick(item_start) == i).astype(I32)
    items = (blk, pick(ids), lo, hi, first, pick(ordinal) % 2, pick(nxt_e),
             n_items.astype(I32).reshape(1))
    return grp_start.astype(I32), items


def kernel(x, c, w_ada, b_ada, norm1_g, w_in, q_norm_g, k_norm_g, conv_w, attn_out_g, conv_out_g,
           w_o, norm2_g, w_router, router_bias, w_gate, w_up, w_down, ws_gate, ws_up, ws_down):
    b, s, d = x.shape
    x2 = x.reshape(b * s, d)

    c8 = jnp.pad(c, ((0, 8 - b), (0, 0)))
    mod3 = _adaln(c8, w_ada, b_ada)[:b].reshape(b, 6, d)

    qk_gain = jnp.concatenate([jnp.tile(q_norm_g[0] * HEAD_DIM ** -0.5, ATTN_HEADS),
                               jnp.tile(k_norm_g[0], ATTN_HEADS)]).reshape(1, 2 * ATTN_WIDTH)
    proj = _in_proj(x2, mod3, norm1_g, w_in, qk_gain)

    a, y = _mixer(proj, conv_w, attn_out_g, conv_out_g)
    x1 = _out_proj(a, y, w_o, x2, mod3)

    w_rt = w_router[0].T
    bias_col = jnp.broadcast_to(router_bias[0][:, None], (N_EXPERTS, LANES))
    idx_t, w_t, pos_t, cnt = _route(x1, norm2_g, mod3, w_rt, bias_col)

    grp_start, items = _work_items(cnt[:, 0])
    xs, base, slot_t = _shared_dispatch(grp_start, idx_t, pos_t, x1, norm2_g, mod3, ws_gate, ws_up, ws_down)
    ys = _experts(items, xs, w_gate, w_up, w_down)
    out = _combine(slot_t, w_t, base, mod3, ys)
    return out.reshape(b, s, d)
```

```python
import functools

import jax
import jax.numpy as jnp
from jax import lax
from jax.experimental import pallas as pl
from jax.experimental.pallas import tpu as pltpu

F32 = jnp.float32
BF16 = jnp.bfloat16
I32 = jnp.int32

D_MODEL = 2048
BATCH = 4
SEQ = 2048
N_TOK = BATCH * SEQ
HEAD_DIM = 128
ATTN_WIDTH = 1024
ATTN_HEADS = 8
CONV_WIDTH = 1024
CONV_GROUPS = 8
IN_WIDTH = 6144
CONV_K = 3
MOBA_BLOCK = 256
MOBA_NB = SEQ // MOBA_BLOCK
MOBA_TOPK = 3
N_EXPERTS = 256
TOP_K = 8
N_GROUPS = 8
GROUP_SIZE = N_EXPERTS // N_GROUPS
TOPK_GROUPS = 4
EXPERT_DIM = 512
SHARED_DIM = 512
ROUTED_SCALE = 2.5
EPS = 1e-6

LANES = 128
SUBLANES = 8
MXU_COLS = 256
MASKED = -1e30
VMEM_LIMIT = 56 * 1024 * 1024

ADA_TN = 1024
PROJ_TM = 1024
PROJ_TN = 512
ROUTE_TM = 256
DISP_TM = 256
MOE_ROWS = 128
N_SLOTS = N_TOK * TOP_K
N_ROW_BLOCKS = N_SLOTS // MOE_ROWS
MAX_ITEMS = N_ROW_BLOCKS + N_EXPERTS
COMB_TM = 128


def _dot(a, b):
    return jnp.dot(a, b, preferred_element_type=F32)


def _dot_nt(a, b):
    return lax.dot_general(a, b, (((1,), (1,)), ((), ())), preferred_element_type=F32)


def _params(sem):
    return pltpu.CompilerParams(dimension_semantics=sem, vmem_limit_bytes=VMEM_LIMIT)


def _adaln_kernel(c_ref, w_ref, b_ref, o_ref):
    c = c_ref[...]
    o_ref[...] = _dot(c * jax.nn.sigmoid(c), w_ref[...]) + b_ref[...]


def _adaln(c8, w_ada, b_ada):
    n = w_ada.shape[-1]
    return pl.pallas_call(
        _adaln_kernel,
        grid=(n // ADA_TN,),
        in_specs=[
            pl.BlockSpec((8, D_MODEL), lambda j: (0, 0)),
            pl.BlockSpec((None, D_MODEL, ADA_TN), lambda j: (0, 0, j)),
            pl.BlockSpec((1, ADA_TN), lambda j: (0, j)),
        ],
        out_specs=pl.BlockSpec((8, ADA_TN), lambda j: (0, j)),
        out_shape=jax.ShapeDtypeStruct((8, n), F32),
        compiler_params=_params(("arbitrary",)),
        name="adaln",
    )(c8, w_ada, b_ada)


def _in_proj_kernel(x_ref, mod_ref, g_ref, w_ref, qkg_ref, o_ref, h_ref):
    j = pl.program_id(1)

    @pl.when(j == 0)
    def _():
        xf = x_ref[...]
        ms = jnp.mean(xf * xf, axis=-1, keepdims=True)
        y = xf * lax.rsqrt(ms + EPS) * g_ref[...]
        h_ref[...] = (y * (1.0 + mod_ref[0, 1:2, :]) + mod_ref[0, 0:1, :]).astype(BF16)

    @pl.when(j < 2 * ATTN_WIDTH // PROJ_TN)
    def _():
        for pair in range(PROJ_TN // MXU_COLS):
            cols = slice(pair * MXU_COLS, (pair + 1) * MXU_COLS)
            acc = _dot(h_ref[...], w_ref[:, cols].astype(BF16))
            for hh in range(MXU_COLS // HEAD_DIM):
                a = acc[:, hh * HEAD_DIM:(hh + 1) * HEAD_DIM]
                sl = slice(pair * MXU_COLS + hh * HEAD_DIM, pair * MXU_COLS + (hh + 1) * HEAD_DIM)
                ms = jnp.mean(a * a, axis=-1, keepdims=True)
                o_ref[:, sl] = (a * lax.rsqrt(ms + EPS) * qkg_ref[:, sl]).astype(BF16)

    @pl.when(j >= 2 * ATTN_WIDTH // PROJ_TN)
    def _():
        o_ref[...] = _dot(h_ref[...], w_ref[...].astype(BF16)).astype(BF16)


def _in_proj(x2, mod3, norm1_g, w_in, qk_gain):
    n_qk = 2 * ATTN_WIDTH // PROJ_TN
    rows_per_batch = SEQ // PROJ_TM
    return pl.pallas_call(
        _in_proj_kernel,
        grid=(N_TOK // PROJ_TM, IN_WIDTH // PROJ_TN),
        in_specs=[
            pl.BlockSpec((PROJ_TM, D_MODEL), lambda i, j: (i, 0)),
            pl.BlockSpec((1, 6, D_MODEL), lambda i, j: (i // rows_per_batch, 0, 0)),
            pl.BlockSpec((1, D_MODEL), lambda i, j: (0, 0)),
            pl.BlockSpec((None, D_MODEL, PROJ_TN), lambda i, j: (0, 0, j)),
            pl.BlockSpec((1, PROJ_TN), lambda i, j: (0, jnp.minimum(j, n_qk - 1))),
        ],
        out_specs=pl.BlockSpec((PROJ_TM, PROJ_TN), lambda i, j: (i, j)),
        out_shape=jax.ShapeDtypeStruct((N_TOK, IN_WIDTH), BF16),
        scratch_shapes=[pltpu.VMEM((PROJ_TM, D_MODEL), BF16)],
        compiler_params=_params(("arbitrary", "arbitrary")),
        name="in_proj",
    )(x2, mod3, norm1_g, w_in, qk_gain)


def _mixer_kernel(q_ref, k_ref, v_ref, u_ref, bg_ref, cg_ref, cw_ref, ag_ref, yg_ref,
                  a_ref, y_ref, s_ref):
    blk = MOBA_BLOCK
    k = k_ref[...]
    v = v_ref[...]

    km = jnp.mean(k.astype(F32).reshape(MOBA_NB, blk, HEAD_DIM), axis=1)
    km_hi = km.astype(BF16)
    km_lo = (km - km_hi.astype(F32)).astype(BF16)
    zpad = jnp.zeros((LANES - MOBA_NB, HEAD_DIM), BF16)
    km_hi = jnp.concatenate([km_hi, zpad], axis=0)
    km_lo = jnp.concatenate([km_lo, zpad], axis=0)

    lane = lax.broadcasted_iota(I32, (blk, LANES), 1)
    row = lax.broadcasted_iota(I32, (blk, blk), 0)
    col = lax.broadcasted_iota(I32, (blk, blk), 1)

    for i in range(MOBA_NB):
        qi = q_ref[i * blk:(i + 1) * blk, :]
        if i > 0:
            if i > MOBA_TOPK:
                g = _dot_nt(qi, km_hi) + _dot_nt(qi, km_lo)
                rank = jnp.zeros((blk, LANES), F32)
                for jp in range(i):
                    cj = g[:, jp:jp + 1]
                    tie = jnp.where(lane > jp, 1.0, 0.0)
                    rank = rank + jnp.where(cj > g, 1.0, jnp.where(cj == g, tie, 0.0))
                sel = jnp.where(rank < MOBA_TOPK, 1.0, 0.0)
            else:
                sel = jnp.ones((blk, LANES), F32)
            for j in range(i):
                s = _dot_nt(qi, k[j * blk:(j + 1) * blk])
                s_ref[:, j * blk:(j + 1) * blk] = jnp.where(sel[:, j:j + 1] > 0.5, s, MASKED)
        s = _dot_nt(qi, k[i * blk:(i + 1) * blk])
        s_ref[:, i * blk:(i + 1) * blk] = jnp.where(col <= row, s, MASKED)

        width = (i + 1) * blk
        sc = s_ref[:, :width]
        m = jnp.max(sc, axis=-1, keepdims=True)
        p = jnp.exp(sc - m)
        denom = jnp.sum(p, axis=-1, keepdims=True)
        o = _dot(p.astype(BF16), v[:width]) / denom
        ms = jnp.mean(o * o, axis=-1, keepdims=True)
        a_ref[i * blk:(i + 1) * blk, :] = (o * lax.rsqrt(ms + EPS) * ag_ref[...]).astype(BF16)

    z = cg_ref[...].astype(F32) * u_ref[...].astype(F32)
    t = lax.broadcasted_iota(I32, z.shape, 0)
    z1 = jnp.where(t >= 1, pltpu.roll(z, 1, 0), 0.0)
    z2 = jnp.where(t >= 2, pltpu.roll(z, 2, 0), 0.0)
    y = cw_ref[0:1, :] * z2 + cw_ref[1:2, :] * z1 + cw_ref[2:3, :] * z
    y = bg_ref[...].astype(F32) * y
    ms = jnp.mean(y * y, axis=-1, keepdims=True)
    y_ref[...] = (y * lax.rsqrt(ms + EPS) * yg_ref[...]).astype(BF16)


def _mixer(proj, conv_w, attn_out_g, conv_out_g):
    h = ATTN_HEADS

    def col(off):
        return pl.BlockSpec((SEQ, HEAD_DIM), lambda b, g: (b, off + g))

    out_spec = pl.BlockSpec((SEQ, HEAD_DIM), lambda b, g: (b, g))
    vec_spec = pl.BlockSpec((1, HEAD_DIM), lambda b, g: (0, g))
    return pl.pallas_call(
        _mixer_kernel,
        grid=(BATCH, h),
        in_specs=[col(0), col(h), col(2 * h), col(3 * h), col(4 * h), col(5 * h),
                  pl.BlockSpec((None, CONV_K, HEAD_DIM), lambda b, g: (0, 0, g)),
                  vec_spec, vec_spec],
        out_specs=[out_spec, out_spec],
        out_shape=[jax.ShapeDtypeStruct((N_TOK, ATTN_WIDTH), BF16),
                   jax.ShapeDtypeStruct((N_TOK, CONV_WIDTH), BF16)],
        scratch_shapes=[pltpu.VMEM((MOBA_BLOCK, SEQ), F32)],
        compiler_params=_params(("arbitrary", "arbitrary")),
        name="mixer",
    )(proj, proj, proj, proj, proj, proj, conv_w, attn_out_g, conv_out_g)


def _out_proj_kernel(a_ref, y_ref, wa_ref, wb_ref, x_ref, mod_ref, o_ref):
    mix = _dot(a_ref[...], wa_ref[...].astype(BF16)) + _dot(y_ref[...], wb_ref[...].astype(BF16))
    o_ref[...] = x_ref[...] + mod_ref[0, 2:3, :] * mix


def _out_proj(a, y, w_o, x2, mod3):
    rows_per_batch = SEQ // PROJ_TM
    return pl.pallas_call(
        _out_proj_kernel,
        grid=(N_TOK // PROJ_TM, D_MODEL // PROJ_TN),
        in_specs=[
            pl.BlockSpec((PROJ_TM, ATTN_WIDTH), lambda i, j: (i, 0)),
            pl.BlockSpec((PROJ_TM, CONV_WIDTH), lambda i, j: (i, 0)),
            pl.BlockSpec((None, ATTN_WIDTH, PROJ_TN), lambda i, j: (0, 0, j)),
            pl.BlockSpec((None, CONV_WIDTH, PROJ_TN), lambda i, j: (0, 1, j)),
            pl.BlockSpec((PROJ_TM, PROJ_TN), lambda i, j: (i, j)),
            pl.BlockSpec((1, 6, PROJ_TN), lambda i, j: (i // rows_per_batch, 0, j)),
        ],
        out_specs=pl.BlockSpec((PROJ_TM, PROJ_TN), lambda i, j: (i, j)),
        out_shape=jax.ShapeDtypeStruct((N_TOK, D_MODEL), F32),
        compiler_params=_params(("arbitrary", "arbitrary")),
        name="out_proj",
    )(a, y, w_o, w_o, x2, mod3)


def _norm2_modulate(xf, g_ref, mod_ref):
    ms = jnp.mean(xf * xf, axis=-1, keepdims=True)
    return xf * lax.rsqrt(ms + EPS) * g_ref[...] * (1.0 + mod_ref[0, 4:5, :]) + mod_ref[0, 3:4, :]


def _route_kernel(x_ref, g_ref, mod_ref, wrt_ref, bias_ref,
                  idx_ref, wt_ref, pos_ref, cnt_ref, carry_ref):
    tm = ROUTE_TM
    ne = N_EXPERTS

    @pl.when(pl.program_id(0) == 0)
    def _():
        carry_ref[...] = jnp.zeros_like(carry_ref)

    hb = _norm2_modulate(x_ref[...], g_ref, mod_ref).astype(BF16)

    scores = jax.nn.sigmoid(_dot_nt(wrt_ref[...].astype(BF16), hb))
    choice = scores + bias_ref[:, 0:1]
    ninf = -jnp.inf

    gi = lax.broadcasted_iota(I32, (GROUP_SIZE, tm), 0)
    rows = []
    for g in range(N_GROUPS):
        blk = choice[g * GROUP_SIZE:(g + 1) * GROUP_SIZE, :]
        m1 = jnp.max(blk, axis=0, keepdims=True)
        i1 = jnp.min(jnp.where(blk == m1, gi, GROUP_SIZE), axis=0, keepdims=True)
        m2 = jnp.max(jnp.where(gi == i1, ninf, blk), axis=0, keepdims=True)
        rows.append(m1 + m2)
    gsc = jnp.concatenate(rows, axis=0)

    gidx = lax.broadcasted_iota(I32, (N_GROUPS, tm), 0)
    rank = jnp.zeros((N_GROUPS, tm), F32)
    for gp in range(N_GROUPS):
        r = gsc[gp:gp + 1, :]
        tie = jnp.where(gidx > gp, 1.0, 0.0)
        rank = rank + jnp.where(r > gsc, 1.0, jnp.where(r == gsc, tie, 0.0))
    gsel = jnp.where(rank < TOPK_GROUPS, 1.0, 0.0)
    esel = jnp.concatenate(
        [jnp.broadcast_to(gsel[g:g + 1, :], (GROUP_SIZE, tm)) for g in range(N_GROUPS)], axis=0)
    masked = jnp.where(esel > 0.5, choice, ninf)

    eidx = lax.broadcasted_iota(I32, (ne, tm), 0)
    idx_rows, w_rows = [], []
    onehot = jnp.zeros((ne, tm), F32)
    for _ in range(TOP_K):
        m = jnp.max(masked, axis=0, keepdims=True)
        sel = jnp.min(jnp.where(masked == m, eidx, ne), axis=0, keepdims=True)
        hit = eidx == sel
        idx_rows.append(sel)
        w_rows.append(jnp.sum(jnp.where(hit, scores, 0.0), axis=0, keepdims=True))
        masked = jnp.where(hit, ninf, masked)
        onehot = jnp.where(hit, 1.0, onehot)
    wsel = jnp.concatenate(w_rows, axis=0)
    idx_ref[...] = jnp.concatenate(idx_rows, axis=0)
    wt_ref[...] = wsel / jnp.sum(wsel, axis=0, keepdims=True) * ROUTED_SCALE

    ti = lax.broadcasted_iota(I32, (tm, tm), 0)
    tj = lax.broadcasted_iota(I32, (tm, tm), 1)
    upper = jnp.where(ti < tj, 1.0, 0.0).astype(BF16)
    before = _dot(onehot.astype(BF16), upper) + carry_ref[:, 0:1]
    pos_rows = [jnp.sum(jnp.where(eidx == idx_rows[kk], before, 0.0), axis=0, keepdims=True)
                for kk in range(TOP_K)]
    pos_ref[...] = jnp.concatenate(pos_rows, axis=0).astype(I32)
    carry_ref[...] = carry_ref[...] + jnp.sum(onehot, axis=1, keepdims=True)
    cnt_ref[...] = carry_ref[...].astype(I32)


def _route(x1, norm2_g, mod3, w_rt, bias_col):
    tm = ROUTE_TM
    rows_per_batch = SEQ // tm
    full = lambda shape: pl.BlockSpec(shape, lambda i: tuple(0 for _ in shape))
    tok_tile = pl.BlockSpec((TOP_K, tm), lambda i: (0, i))
    return pl.pallas_call(
        _route_kernel,
        grid=(N_TOK // tm,),
        in_specs=[
            pl.BlockSpec((tm, D_MODEL), lambda i: (i, 0)),
            full((1, D_MODEL)),
            pl.BlockSpec((1, 6, D_MODEL), lambda i: (i // rows_per_batch, 0, 0)),
            full((N_EXPERTS, D_MODEL)),
            full((N_EXPERTS, LANES)),
        ],
        out_specs=[tok_tile, tok_tile, tok_tile, full((N_EXPERTS, LANES))],
        out_shape=[
            jax.ShapeDtypeStruct((TOP_K, N_TOK), I32),
            jax.ShapeDtypeStruct((TOP_K, N_TOK), F32),
            jax.ShapeDtypeStruct((TOP_K, N_TOK), I32),
            jax.ShapeDtypeStruct((N_EXPERTS, LANES), I32),
        ],
        scratch_shapes=[pltpu.VMEM((N_EXPERTS, LANES), F32)],
        compiler_params=_params(("arbitrary",)),
        name="route",
    )(x1, norm2_g, mod3, w_rt, bias_col)


def _shared_kernel(x_ref, g_ref, mod_ref, wsg_ref, wsu_ref, wsd_ref, hf_ref, base_ref, wsg_bf, wsu_bf, wsd_bf):
    @pl.when(pl.program_id(0) == 0)
    def _():
        wsg_bf[...] = wsg_ref[...].astype(BF16)
        wsu_bf[...] = wsu_ref[...].astype(BF16)
        wsd_bf[...] = wsd_ref[...].astype(BF16)

    xf = x_ref[...]
    hf = _norm2_modulate(xf, g_ref, mod_ref)
    hf_ref[...] = hf

    hb = hf.astype(BF16)
    hg = _dot(hb, wsg_bf[...])
    hu = _dot(hb, wsu_bf[...])
    hid = (hg * jax.nn.sigmoid(hg)) * hu
    shared = _dot(hid.astype(BF16), wsd_bf[...])
    base_ref[...] = xf + mod_ref[0, 5:6, :] * shared


def _shared(x1, norm2_g, mod3, ws_gate, ws_up, ws_down):
    tm = DISP_TM
    rows_per_batch = SEQ // tm
    row_tile = pl.BlockSpec((tm, D_MODEL), lambda i: (i, 0))
    return pl.pallas_call(
        _shared_kernel,
        grid=(N_TOK // tm,),
        in_specs=[
            row_tile,
            pl.BlockSpec((1, D_MODEL), lambda i: (0, 0)),
            pl.BlockSpec((1, 6, D_MODEL), lambda i: (i // rows_per_batch, 0, 0)),
            pl.BlockSpec((None, D_MODEL, SHARED_DIM), lambda i: (0, 0, 0)),
            pl.BlockSpec((None, D_MODEL, SHARED_DIM), lambda i: (0, 0, 0)),
            pl.BlockSpec((None, SHARED_DIM, D_MODEL), lambda i: (0, 0, 0)),
        ],
        out_specs=[row_tile, row_tile],
        out_shape=[jax.ShapeDtypeStruct((N_TOK, D_MODEL), F32),
                   jax.ShapeDtypeStruct((N_TOK, D_MODEL), F32)],
        scratch_shapes=[
            pltpu.VMEM((D_MODEL, SHARED_DIM), BF16),
            pltpu.VMEM((D_MODEL, SHARED_DIM), BF16),
            pltpu.VMEM((SHARED_DIM, D_MODEL), BF16),
        ],
        compiler_params=_params(("arbitrary",)),
        name="shared",
    )(x1, norm2_g, mod3, ws_gate, ws_up, ws_down)


DIGIT_BITS = 7
ROW_BITS = 7


def _invperm_kernel(start_ref, idx_ref, pos_ref, tok_ref, dst_ref, acc_ref):
    tm = ROUTE_TM
    step = pl.program_id(0)

    @pl.when(step == 0)
    def _():
        acc_ref[...] = jnp.zeros_like(acc_ref)

    eidx = lax.broadcasted_iota(I32, (N_EXPERTS, tm), 0)
    bidx = lax.broadcasted_iota(I32, (N_ROW_BLOCKS, tm), 0)
    ridx = lax.broadcasted_iota(I32, (MOE_ROWS, tm), 0)
    tok = step * tm + lax.broadcasted_iota(I32, (1, tm), 1)
    start = start_ref[:, 0:1]
    n_digits = -(-(TOP_K * N_TOK).bit_length() // DIGIT_BITS)

    acc = acc_ref[...]
    for kk in range(TOP_K):
        first = jnp.sum(jnp.where(eidx == idx_ref[kk:kk + 1, :], start, 0.0), axis=0, keepdims=True)
        slot = first.astype(I32) + pos_ref[kk:kk + 1, :]
        in_blk = bidx == (slot >> ROW_BITS)
        in_row = jnp.where(ridx == (slot & (MOE_ROWS - 1)), 1.0, 0.0).astype(BF16)
        code = kk * N_TOK + tok + 1
        for d in range(n_digits):
            digit = ((code >> (d * DIGIT_BITS)) & ((1 << DIGIT_BITS) - 1)).astype(F32)
            part = _dot_nt(jnp.where(in_blk, digit, 0.0).astype(BF16), in_row)
            acc = acc + float(1 << (d * DIGIT_BITS)) * part
    acc_ref[...] = acc
    dst = acc.astype(I32) - 1
    dst_ref[...] = dst
    tok_ref[...] = dst & (N_TOK - 1)


def _invperm(start_col, idx_t, pos_t):
    tm = ROUTE_TM
    tok_tile = pl.BlockSpec((TOP_K, tm), lambda i: (0, i))
    table = pl.BlockSpec((N_ROW_BLOCKS, MOE_ROWS), lambda i: (0, 0))
    table_shape = jax.ShapeDtypeStruct((N_ROW_BLOCKS, MOE_ROWS), I32)
    return pl.pallas_call(
        _invperm_kernel,
        grid=(N_TOK // tm,),
        in_specs=[pl.BlockSpec((N_EXPERTS, LANES), lambda i: (0, 0)), tok_tile, tok_tile],
        out_specs=[table, table],
        out_shape=[table_shape, table_shape],
        scratch_shapes=[pltpu.VMEM((N_ROW_BLOCKS, MOE_ROWS), F32)],
        compiler_params=_params(("arbitrary",)),
        name="invperm",
    )(start_col, idx_t, pos_t)


def _weight_copies(hbm_refs, expert, bufs, slot, sems):
    return [pltpu.make_async_copy(h.at[0, expert], b.at[slot], sems.at[slot, j])
            for j, (h, b) in enumerate(zip(hbm_refs, bufs))]


def _for_each_row(fn):
    def tile(r8, carry):
        r0 = pl.multiple_of(r8 * SUBLANES, SUBLANES)
        for j in range(SUBLANES):
            fn(r0 + j)
        return carry

    lax.fori_loop(0, MOE_ROWS // SUBLANES, tile, 0)


def _experts_kernel(blk_ref, exp_ref, lo_ref, hi_ref, first_ref, par_ref, nxt_ref, n_ref,
                    tok_ref, tok_next_ref, dst_ref, dst_prev_ref, hf_hbm, wg_hbm, wu_hbm, wd_hbm, ys_hbm,
                    x_buf, o_buf, wg_buf, wu_buf, wd_buf, wg_bf, wu_bf, wd_bf, sems, gsem, ssem):
    i = pl.program_id(0)
    hbm = (wg_hbm, wu_hbm, wd_hbm)
    bufs = (wg_buf, wu_buf, wd_buf)
    n_items = n_ref[0]
    active = i < n_items
    expert = exp_ref[i]
    slot = par_ref[i]
    blk = blk_ref[i]
    bslot = blk & 1
    lo = lo_ref[i]
    hi = hi_ref[i]

    def gather_copy(tok, r, s):
        return pltpu.make_async_copy(hf_hbm.at[pl.ds(tok, 1), :], x_buf.at[s, pl.ds(r, 1), :], gsem.at[s])

    def scatter_copy(r, dst, s):
        return pltpu.make_async_copy(o_buf.at[s, pl.ds(r, 1), :], ys_hbm.at[pl.ds(dst, 1), :], ssem.at[s])

    def start_gather(table, s):
        _for_each_row(lambda r: gather_copy(table[0, 0, r], r, s).start())

    def wait_gather(s):
        _for_each_row(lambda r: gather_copy(0, 0, s).wait())

    def wait_scatter(s):
        _for_each_row(lambda r: scatter_copy(0, 0, s).wait())

    @pl.when(i == 0)
    def _():
        for cp in _weight_copies(hbm, expert, bufs, 0, sems):
            cp.start()
        start_gather(tok_ref, bslot)

    first_of_block = jnp.logical_and(active, lo == 0)
    interior = jnp.logical_and(blk >= 1, blk + 1 < N_ROW_BLOCKS)

    @pl.when(first_of_block)
    def _():
        wait_gather(bslot)

        @pl.when(blk >= 2)
        def _():
            wait_scatter(bslot)

        @pl.when(blk == 0)
        def _():
            start_gather(tok_next_ref, 1 - bslot)

        @pl.when(blk == N_ROW_BLOCKS - 1)
        def _():
            _for_each_row(lambda r: scatter_copy(r, dst_prev_ref[0, 0, r], 1 - bslot).start())

    @pl.when(jnp.logical_and(active, first_ref[i] == 1))
    def _():
        for cp in _weight_copies(hbm, expert, bufs, slot, sems):
            cp.wait()
        nxt = nxt_ref[i]

        @pl.when(nxt >= 0)
        def _():
            for cp in _weight_copies(hbm, nxt, bufs, 1 - slot, sems):
                cp.start(priority=1)

        wg_bf[...] = wg_buf[slot].astype(BF16)
        wu_bf[...] = wu_buf[slot].astype(BF16)
        wd_bf[...] = wd_buf[slot].astype(BF16)

    def ffn(first_owner):
        x = x_buf[bslot].astype(BF16)
        hg = _dot(x, wg_bf[...])
        hu = _dot(x, wu_bf[...])
        hid = (hg * jax.nn.sigmoid(hg)) * hu
        y = _dot(hid.astype(BF16), wd_bf[...])
        r = lax.broadcasted_iota(I32, (MOE_ROWS, 1), 0)
        mine = (r >= lo) & (r < hi)
        o_buf[bslot] = jnp.where(mine, y, 0.0 if first_owner else o_buf[bslot])

    @pl.when(jnp.logical_and(first_of_block, interior))
    def _():
        for r in range(MOE_ROWS):
            scatter_copy(r, dst_prev_ref[0, 0, r], 1 - bslot).start()
            gather_copy(tok_next_ref[0, 0, r], r, 1 - bslot).start()
        ffn(True)

    @pl.when(jnp.logical_and(first_of_block, jnp.logical_not(interior)))
    def _():
        ffn(True)

    @pl.when(jnp.logical_and(active, lo > 0))
    def _():
        ffn(False)

    @pl.when(i == n_items - 1)
    def _():
        _for_each_row(lambda r: scatter_copy(r, dst_ref[0, 0, r], bslot).start())
        wait_scatter(1 - bslot)
        wait_scatter(bslot)


def _experts(items, tok, dst, hf, w_gate, w_up, w_down):
    table = lambda step: pl.BlockSpec(
        (1, 1, MOE_ROWS), lambda i, blk, *_: (jnp.clip(blk[i] + step, 0, N_ROW_BLOCKS - 1), 0, 0),
        memory_space=pltpu.SMEM)
    hbm = pl.BlockSpec(memory_space=pl.ANY)
    grid_spec = pltpu.PrefetchScalarGridSpec(
        num_scalar_prefetch=len(items),
        grid=(MAX_ITEMS,),
        in_specs=[table(0), table(1), table(0), table(-1), hbm, hbm, hbm, hbm],
        out_specs=hbm,
        scratch_shapes=[
            pltpu.VMEM((2, MOE_ROWS, D_MODEL), F32),
            pltpu.VMEM((2, MOE_ROWS, D_MODEL), F32),
            pltpu.VMEM((2, D_MODEL, EXPERT_DIM), F32),
            pltpu.VMEM((2, D_MODEL, EXPERT_DIM), F32),
            pltpu.VMEM((2, EXPERT_DIM, D_MODEL), F32),
            pltpu.VMEM((D_MODEL, EXPERT_DIM), BF16),
            pltpu.VMEM((D_MODEL, EXPERT_DIM), BF16),
            pltpu.VMEM((EXPERT_DIM, D_MODEL), BF16),
            pltpu.SemaphoreType.DMA((2, 3)),
            pltpu.SemaphoreType.DMA((2,)),
            pltpu.SemaphoreType.DMA((2,)),
        ],
    )
    tok3 = tok.reshape(N_ROW_BLOCKS, 1, MOE_ROWS)
    dst3 = dst.reshape(N_ROW_BLOCKS, 1, MOE_ROWS)
    return pl.pallas_call(
        _experts_kernel,
        grid_spec=grid_spec,
        out_shape=jax.ShapeDtypeStruct((N_SLOTS, D_MODEL), F32),
        compiler_params=_params(("arbitrary",)),
        name="experts",
    )(*items, tok3, tok3, dst3, dst3, hf, w_gate, w_up, w_down)


def _combine_kernel(wt_ref, base_ref, mod_ref, ys_ref, o_ref):
    w_tok = wt_ref[...].T
    acc = w_tok[:, 0:1] * ys_ref[0]
    for kk in range(1, TOP_K):
        acc = acc + w_tok[:, kk:kk + 1] * ys_ref[kk]
    o_ref[...] = base_ref[...] + mod_ref[0, 5:6, :] * acc


def _combine(w_t, base, mod3, ys):
    tm = COMB_TM
    rows_per_batch = SEQ // tm
    return pl.pallas_call(
        _combine_kernel,
        grid=(N_TOK // tm,),
        in_specs=[
            pl.BlockSpec((TOP_K, tm), lambda i: (0, i)),
            pl.BlockSpec((tm, D_MODEL), lambda i: (i, 0)),
            pl.BlockSpec((1, 6, D_MODEL), lambda i: (i // rows_per_batch, 0, 0)),
            pl.BlockSpec((TOP_K, tm, D_MODEL), lambda i: (0, i, 0)),
        ],
        out_specs=pl.BlockSpec((tm, D_MODEL), lambda i: (i, 0)),
        out_shape=jax.ShapeDtypeStruct((N_TOK, D_MODEL), F32),
        compiler_params=_params(("arbitrary",)),
        name="combine",
    )(w_t, base, mod3, ys.reshape(TOP_K, N_TOK, D_MODEL))


def _work_items(counts):
    rows = MOE_ROWS
    grp_end = jnp.cumsum(counts)
    grp_start = grp_end - counts
    first_blk = grp_start // rows
    last_blk = (grp_end - 1) // rows
    n_blk = jnp.where(counts > 0, last_blk - first_blk + 1, 0)
    item_end = jnp.cumsum(n_blk)
    item_start = item_end - n_blk
    n_items = item_end[-1]

    ids = jnp.arange(N_EXPERTS, dtype=I32)
    nonempty = (n_blk > 0).astype(I32)
    ordinal = jnp.cumsum(nonempty) - nonempty
    later = lax.cummin(jnp.where(nonempty > 0, ids, N_EXPERTS), axis=0, reverse=True)
    nxt_e = jnp.concatenate([later[1:], jnp.full((1,), N_EXPERTS, I32)])
    nxt_e = jnp.where(nxt_e >= N_EXPERTS, -1, nxt_e)

    i = jnp.minimum(jnp.arange(MAX_ITEMS, dtype=I32), n_items - 1)
    own = (item_start[None, :] <= i[:, None]) & (i[:, None] < item_end[None, :])
    pick = lambda v: jnp.sum(jnp.where(own, v[None, :], 0), axis=1).astype(I32)
    blk = pick(first_blk - item_start) + i
    lo = jnp.maximum(pick(grp_start), blk * rows) - blk * rows
    hi = jnp.minimum(pick(grp_end), (blk + 1) * rows) - blk * rows
    first = (pick(item_start) == i).astype(I32)
    items = (blk, pick(ids), lo, hi, first, pick(ordinal) % 2, pick(nxt_e),
             n_items.astype(I32).reshape(1))
    return grp_start.astype(I32), items


def kernel(x, c, w_ada, b_ada, norm1_g, w_in, q_norm_g, k_norm_g, conv_w, attn_out_g, conv_out_g,
           w_o, norm2_g, w_router, router_bias, w_gate, w_up, w_down, ws_gate, ws_up, ws_down):
    b, s, d = x.shape
    x2 = x.reshape(b * s, d)

    c8 = jnp.pad(c, ((0, 8 - b), (0, 0)))
    mod3 = _adaln(c8, w_ada, b_ada)[:b].reshape(b, 6, d)

    qk_gain = jnp.concatenate([jnp.tile(q_norm_g[0] * HEAD_DIM ** -0.5, ATTN_HEADS),
                               jnp.tile(k_norm_g[0], ATTN_HEADS)]).reshape(1, 2 * ATTN_WIDTH)
    proj = _in_proj(x2, mod3, norm1_g, w_in, qk_gain)

    a, y = _mixer(proj, conv_w, attn_out_g, conv_out_g)
    x1 = _out_proj(a, y, w_o, x2, mod3)

    w_rt = w_router[0].T
    bias_col = jnp.broadcast_to(router_bias[0][:, None], (N_EXPERTS, LANES))
    idx_t, w_t, pos_t, cnt = _route(x1, norm2_g, mod3, w_rt, bias_col)

    grp_start, items = _work_items(cnt[:, 0])
    start_col = jnp.broadcast_to(grp_start.astype(F32)[:, None], (N_EXPERTS, LANES))
    slot_tok, slot_dst = _invperm(start_col, idx_t, pos_t)
    hf, base = _shared(x1, norm2_g, mod3, ws_gate, ws_up, ws_down)
    ys = _experts(items, slot_tok, slot_dst, hf, w_gate, w_up, w_down)
    out = _combine(w_t, base, mod3, ys)
    return out.reshape(b, s, d)
```

```python
import functools

import jax
import jax.numpy as jnp
from jax import lax
from jax.experimental import pallas as pl
from jax.experimental.pallas import tpu as pltpu

F32 = jnp.float32
BF16 = jnp.bfloat16
I32 = jnp.int32
U32 = jnp.uint32

D_MODEL = 2048
BATCH = 4
SEQ = 2048
N_TOK = BATCH * SEQ
HEAD_DIM = 128
ATTN_WIDTH = 1024
ATTN_HEADS = 8
CONV_WIDTH = 1024
CONV_GROUPS = 8
IN_WIDTH = 6144
CONV_K = 3
MOBA_BLOCK = 256
MOBA_NB = SEQ // MOBA_BLOCK
MOBA_TOPK = 3
N_EXPERTS = 256
TOP_K = 8
N_GROUPS = 8
GROUP_SIZE = N_EXPERTS // N_GROUPS
TOPK_GROUPS = 4
EXPERT_DIM = 512
SHARED_DIM = 512
ROUTED_SCALE = 2.5
EPS = 1e-6

LANES = 128
SUBLANES = 8
MXU_COLS = 256
MASKED = -1e30
VMEM_LIMIT = 56 * 1024 * 1024

ADA_TN = 1024
PROJ_TM = 1024
PROJ_TN = 512
ROUTE_TM = 256
DISP_TM = 256
MOE_ROWS = 128
N_SLOTS = N_TOK * TOP_K
N_ROW_BLOCKS = N_SLOTS // MOE_ROWS
MAX_ITEMS = N_ROW_BLOCKS + N_EXPERTS
COMB_TM = 256


def _dot(a, b):
    return jnp.dot(a, b, preferred_element_type=F32)


def _dot_nt(a, b):
    return lax.dot_general(a, b, (((1,), (1,)), ((), ())), preferred_element_type=F32)


def _params(sem):
    return pltpu.CompilerParams(dimension_semantics=sem, vmem_limit_bytes=VMEM_LIMIT)


PACKED = D_MODEL // 2


def _pack_row_halves(v):
    lo = pltpu.bitcast(v[:, :PACKED].astype(BF16).astype(F32), U32)
    hi = pltpu.bitcast(v[:, PACKED:].astype(BF16).astype(F32), U32)
    return hi | (lo >> 16)


def _unpack_row_halves(w):
    lo = pltpu.bitcast(w << 16, F32)
    hi = pltpu.bitcast(w & jnp.uint32(0xFFFF0000), F32)
    return lo, hi


def _adaln_kernel(c_ref, w_ref, b_ref, o_ref):
    c = c_ref[...]
    o_ref[...] = _dot(c * jax.nn.sigmoid(c), w_ref[...]) + b_ref[...]


def _adaln(c8, w_ada, b_ada):
    n = w_ada.shape[-1]
    return pl.pallas_call(
        _adaln_kernel,
        grid=(n // ADA_TN,),
        in_specs=[
            pl.BlockSpec((8, D_MODEL), lambda j: (0, 0)),
            pl.BlockSpec((None, D_MODEL, ADA_TN), lambda j: (0, 0, j)),
            pl.BlockSpec((1, ADA_TN), lambda j: (0, j)),
        ],
        out_specs=pl.BlockSpec((8, ADA_TN), lambda j: (0, j)),
        out_shape=jax.ShapeDtypeStruct((8, n), F32),
        compiler_params=_params(("arbitrary",)),
        name="adaln",
    )(c8, w_ada, b_ada)


def _in_proj_kernel(x_ref, mod_ref, g_ref, w_ref, qkg_ref, o_ref, h_ref):
    j = pl.program_id(1)

    @pl.when(j == 0)
    def _():
        xf = x_ref[...]
        ms = jnp.mean(xf * xf, axis=-1, keepdims=True)
        y = xf * lax.rsqrt(ms + EPS) * g_ref[...]
        h_ref[...] = (y * (1.0 + mod_ref[0, 1:2, :]) + mod_ref[0, 0:1, :]).astype(BF16)

    @pl.when(j < 2 * ATTN_WIDTH // PROJ_TN)
    def _():
        for pair in range(PROJ_TN // MXU_COLS):
            cols = slice(pair * MXU_COLS, (pair + 1) * MXU_COLS)
            acc = _dot(h_ref[...], w_ref[:, cols].astype(BF16))
            for hh in range(MXU_COLS // HEAD_DIM):
                a = acc[:, hh * HEAD_DIM:(hh + 1) * HEAD_DIM]
                sl = slice(pair * MXU_COLS + hh * HEAD_DIM, pair * MXU_COLS + (hh + 1) * HEAD_DIM)
                ms = jnp.mean(a * a, axis=-1, keepdims=True)
                o_ref[:, sl] = (a * lax.rsqrt(ms + EPS) * qkg_ref[:, sl]).astype(BF16)

    @pl.when(j >= 2 * ATTN_WIDTH // PROJ_TN)
    def _():
        o_ref[...] = _dot(h_ref[...], w_ref[...].astype(BF16)).astype(BF16)


def _in_proj(x2, mod3, norm1_g, w_in, qk_gain):
    n_qk = 2 * ATTN_WIDTH // PROJ_TN
    rows_per_batch = SEQ // PROJ_TM
    return pl.pallas_call(
        _in_proj_kernel,
        grid=(N_TOK // PROJ_TM, IN_WIDTH // PROJ_TN),
        in_specs=[
            pl.BlockSpec((PROJ_TM, D_MODEL), lambda i, j: (i, 0)),
            pl.BlockSpec((1, 6, D_MODEL), lambda i, j: (i // rows_per_batch, 0, 0)),
            pl.BlockSpec((1, D_MODEL), lambda i, j: (0, 0)),
            pl.BlockSpec((None, D_MODEL, PROJ_TN), lambda i, j: (0, 0, j)),
            pl.BlockSpec((1, PROJ_TN), lambda i, j: (0, jnp.minimum(j, n_qk - 1))),
        ],
        out_specs=pl.BlockSpec((PROJ_TM, PROJ_TN), lambda i, j: (i, j)),
        out_shape=jax.ShapeDtypeStruct((N_TOK, IN_WIDTH), BF16),
        scratch_shapes=[pltpu.VMEM((PROJ_TM, D_MODEL), BF16)],
        compiler_params=_params(("arbitrary", "arbitrary")),
        name="in_proj",
    )(x2, mod3, norm1_g, w_in, qk_gain)


def _mixer_kernel(q_ref, k_ref, v_ref, u_ref, bg_ref, cg_ref, cw_ref, ag_ref, yg_ref,
                  a_ref, y_ref, s_ref):
    blk = MOBA_BLOCK
    k = k_ref[...]
    v = v_ref[...]

    km = jnp.mean(k.astype(F32).reshape(MOBA_NB, blk, HEAD_DIM), axis=1)
    km_hi = km.astype(BF16)
    km_lo = (km - km_hi.astype(F32)).astype(BF16)
    zpad = jnp.zeros((LANES - MOBA_NB, HEAD_DIM), BF16)
    km_hi = jnp.concatenate([km_hi, zpad], axis=0)
    km_lo = jnp.concatenate([km_lo, zpad], axis=0)

    lane = lax.broadcasted_iota(I32, (blk, LANES), 1)
    row = lax.broadcasted_iota(I32, (blk, blk), 0)
    col = lax.broadcasted_iota(I32, (blk, blk), 1)

    for i in range(MOBA_NB):
        qi = q_ref[i * blk:(i + 1) * blk, :]
        if i > 0:
            if i > MOBA_TOPK:
                g = _dot_nt(qi, km_hi) + _dot_nt(qi, km_lo)
                rank = jnp.zeros((blk, LANES), F32)
                for jp in range(i):
                    cj = g[:, jp:jp + 1]
                    tie = jnp.where(lane > jp, 1.0, 0.0)
                    rank = rank + jnp.where(cj > g, 1.0, jnp.where(cj == g, tie, 0.0))
                sel = jnp.where(rank < MOBA_TOPK, 1.0, 0.0)
            else:
                sel = jnp.ones((blk, LANES), F32)
            for j in range(i):
                s = _dot_nt(qi, k[j * blk:(j + 1) * blk])
                s_ref[:, j * blk:(j + 1) * blk] = jnp.where(sel[:, j:j + 1] > 0.5, s, MASKED)
        s = _dot_nt(qi, k[i * blk:(i + 1) * blk])
        s_ref[:, i * blk:(i + 1) * blk] = jnp.where(col <= row, s, MASKED)

        width = (i + 1) * blk
        sc = s_ref[:, :width]
        m = jnp.max(sc, axis=-1, keepdims=True)
        p = jnp.exp(sc - m)
        denom = jnp.sum(p, axis=-1, keepdims=True)
        o = _dot(p.astype(BF16), v[:width]) / denom
        ms = jnp.mean(o * o, axis=-1, keepdims=True)
        a_ref[i * blk:(i + 1) * blk, :] = (o * lax.rsqrt(ms + EPS) * ag_ref[...]).astype(BF16)

    z = cg_ref[...].astype(F32) * u_ref[...].astype(F32)
    t = lax.broadcasted_iota(I32, z.shape, 0)
    z1 = jnp.where(t >= 1, pltpu.roll(z, 1, 0), 0.0)
    z2 = jnp.where(t >= 2, pltpu.roll(z, 2, 0), 0.0)
    y = cw_ref[0:1, :] * z2 + cw_ref[1:2, :] * z1 + cw_ref[2:3, :] * z
    y = bg_ref[...].astype(F32) * y
    ms = jnp.mean(y * y, axis=-1, keepdims=True)
    y_ref[...] = (y * lax.rsqrt(ms + EPS) * yg_ref[...]).astype(BF16)


def _mixer(proj, conv_w, attn_out_g, conv_out_g):
    h = ATTN_HEADS

    def col(off):
        return pl.BlockSpec((SEQ, HEAD_DIM), lambda b, g: (b, off + g))

    out_spec = pl.BlockSpec((SEQ, HEAD_DIM), lambda b, g: (b, g))
    vec_spec = pl.BlockSpec((1, HEAD_DIM), lambda b, g: (0, g))
    return pl.pallas_call(
        _mixer_kernel,
        grid=(BATCH, h),
        in_specs=[col(0), col(h), col(2 * h), col(3 * h), col(4 * h), col(5 * h),
                  pl.BlockSpec((None, CONV_K, HEAD_DIM), lambda b, g: (0, 0, g)),
                  vec_spec, vec_spec],
        out_specs=[out_spec, out_spec],
        out_shape=[jax.ShapeDtypeStruct((N_TOK, ATTN_WIDTH), BF16),
                   jax.ShapeDtypeStruct((N_TOK, CONV_WIDTH), BF16)],
        scratch_shapes=[pltpu.VMEM((MOBA_BLOCK, SEQ), F32)],
        compiler_params=_params(("arbitrary", "arbitrary")),
        name="mixer",
    )(proj, proj, proj, proj, proj, proj, conv_w, attn_out_g, conv_out_g)


def _out_proj_kernel(a_ref, y_ref, wa_ref, wb_ref, x_ref, mod_ref, o_ref):
    mix = _dot(a_ref[...], wa_ref[...].astype(BF16)) + _dot(y_ref[...], wb_ref[...].astype(BF16))
    o_ref[...] = x_ref[...] + mod_ref[0, 2:3, :] * mix


def _out_proj(a, y, w_o, x2, mod3):
    rows_per_batch = SEQ // PROJ_TM
    return pl.pallas_call(
        _out_proj_kernel,
        grid=(N_TOK // PROJ_TM, D_MODEL // PROJ_TN),
        in_specs=[
            pl.BlockSpec((PROJ_TM, ATTN_WIDTH), lambda i, j: (i, 0)),
            pl.BlockSpec((PROJ_TM, CONV_WIDTH), lambda i, j: (i, 0)),
            pl.BlockSpec((None, ATTN_WIDTH, PROJ_TN), lambda i, j: (0, 0, j)),
            pl.BlockSpec((None, CONV_WIDTH, PROJ_TN), lambda i, j: (0, 1, j)),
            pl.BlockSpec((PROJ_TM, PROJ_TN), lambda i, j: (i, j)),
            pl.BlockSpec((1, 6, PROJ_TN), lambda i, j: (i // rows_per_batch, 0, j)),
        ],
        out_specs=pl.BlockSpec((PROJ_TM, PROJ_TN), lambda i, j: (i, j)),
        out_shape=jax.ShapeDtypeStruct((N_TOK, D_MODEL), F32),
        compiler_params=_params(("arbitrary", "arbitrary")),
        name="out_proj",
    )(a, y, w_o, w_o, x2, mod3)


def _norm2_modulate(xf, g_ref, mod_ref):
    ms = jnp.mean(xf * xf, axis=-1, keepdims=True)
    return xf * lax.rsqrt(ms + EPS) * g_ref[...] * (1.0 + mod_ref[0, 4:5, :]) + mod_ref[0, 3:4, :]


def _route_kernel(x_ref, g_ref, mod_ref, wrt_ref, bias_ref,
                  idx_ref, wt_ref, pos_ref, cnt_ref, carry_ref):
    tm = ROUTE_TM
    ne = N_EXPERTS

    @pl.when(pl.program_id(0) == 0)
    def _():
        carry_ref[...] = jnp.zeros_like(carry_ref)

    hb = _norm2_modulate(x_ref[...], g_ref, mod_ref).astype(BF16)

    scores = jax.nn.sigmoid(_dot_nt(wrt_ref[...].astype(BF16), hb))
    choice = scores + bias_ref[:, 0:1]
    ninf = -jnp.inf

    gi = lax.broadcasted_iota(I32, (GROUP_SIZE, tm), 0)
    rows = []
    for g in range(N_GROUPS):
        blk = choice[g * GROUP_SIZE:(g + 1) * GROUP_SIZE, :]
        m1 = jnp.max(blk, axis=0, keepdims=True)
        i1 = jnp.min(jnp.where(blk == m1, gi, GROUP_SIZE), axis=0, keepdims=True)
        m2 = jnp.max(jnp.where(gi == i1, ninf, blk), axis=0, keepdims=True)
        rows.append(m1 + m2)
    gsc = jnp.concatenate(rows, axis=0)

    gidx = lax.broadcasted_iota(I32, (N_GROUPS, tm), 0)
    rank = jnp.zeros((N_GROUPS, tm), F32)
    for gp in range(N_GROUPS):
        r = gsc[gp:gp + 1, :]
        tie = jnp.where(gidx > gp, 1.0, 0.0)
        rank = rank + jnp.where(r > gsc, 1.0, jnp.where(r == gsc, tie, 0.0))
    gsel = jnp.where(rank < TOPK_GROUPS, 1.0, 0.0)
    esel = jnp.concatenate(
        [jnp.broadcast_to(gsel[g:g + 1, :], (GROUP_SIZE, tm)) for g in range(N_GROUPS)], axis=0)
    masked = jnp.where(esel > 0.5, choice, ninf)

    eidx = lax.broadcasted_iota(I32, (ne, tm), 0)
    idx_rows, w_rows = [], []
    onehot = jnp.zeros((ne, tm), F32)
    for _ in range(TOP_K):
        m = jnp.max(masked, axis=0, keepdims=True)
        sel = jnp.min(jnp.where(masked == m, eidx, ne), axis=0, keepdims=True)
        hit = eidx == sel
        idx_rows.append(sel)
        w_rows.append(jnp.sum(jnp.where(hit, scores, 0.0), axis=0, keepdims=True))
        masked = jnp.where(hit, ninf, masked)
        onehot = jnp.where(hit, 1.0, onehot)
    wsel = jnp.concatenate(w_rows, axis=0)
    idx_ref[...] = jnp.concatenate(idx_rows, axis=0)
    wt_ref[...] = wsel / jnp.sum(wsel, axis=0, keepdims=True) * ROUTED_SCALE

    ti = lax.broadcasted_iota(I32, (tm, tm), 0)
    tj = lax.broadcasted_iota(I32, (tm, tm), 1)
    upper = jnp.where(ti < tj, 1.0, 0.0).astype(BF16)
    before = _dot(onehot.astype(BF16), upper) + carry_ref[:, 0:1]
    pos_rows = [jnp.sum(jnp.where(eidx == idx_rows[kk], before, 0.0), axis=0, keepdims=True)
                for kk in range(TOP_K)]
    pos_ref[...] = jnp.concatenate(pos_rows, axis=0).astype(I32)
    carry_ref[...] = carry_ref[...] + jnp.sum(onehot, axis=1, keepdims=True)
    cnt_ref[...] = carry_ref[...].astype(I32)


def _route(x1, norm2_g, mod3, w_rt, bias_col):
    tm = ROUTE_TM
    rows_per_batch = SEQ // tm
    full = lambda shape: pl.BlockSpec(shape, lambda i: tuple(0 for _ in shape))
    tok_tile = pl.BlockSpec((TOP_K, tm), lambda i: (0, i))
    return pl.pallas_call(
        _route_kernel,
        grid=(N_TOK // tm,),
        in_specs=[
            pl.BlockSpec((tm, D_MODEL), lambda i: (i, 0)),
            full((1, D_MODEL)),
            pl.BlockSpec((1, 6, D_MODEL), lambda i: (i // rows_per_batch, 0, 0)),
            full((N_EXPERTS, D_MODEL)),
            full((N_EXPERTS, LANES)),
        ],
        out_specs=[tok_tile, tok_tile, tok_tile, full((N_EXPERTS, LANES))],
        out_shape=[
            jax.ShapeDtypeStruct((TOP_K, N_TOK), I32),
            jax.ShapeDtypeStruct((TOP_K, N_TOK), F32),
            jax.ShapeDtypeStruct((TOP_K, N_TOK), I32),
            jax.ShapeDtypeStruct((N_EXPERTS, LANES), I32),
        ],
        scratch_shapes=[pltpu.VMEM((N_EXPERTS, LANES), F32)],
        compiler_params=_params(("arbitrary",)),
        name="route",
    )(x1, norm2_g, mod3, w_rt, bias_col)


def _shared_kernel(x_ref, g_ref, mod_ref, wsg_ref, wsu_ref, wsd_ref, hf_ref, base_ref, wsg_bf, wsu_bf, wsd_bf):
    @pl.when(pl.program_id(0) == 0)
    def _():
        wsg_bf[...] = wsg_ref[...].astype(BF16)
        wsu_bf[...] = wsu_ref[...].astype(BF16)
        wsd_bf[...] = wsd_ref[...].astype(BF16)

    xf = x_ref[...]
    hf = _norm2_modulate(xf, g_ref, mod_ref)
    hf_ref[...] = _pack_row_halves(hf)

    hb = hf.astype(BF16)
    hg = _dot(hb, wsg_bf[...])
    hu = _dot(hb, wsu_bf[...])
    hid = (hg * jax.nn.sigmoid(hg)) * hu
    shared = _dot(hid.astype(BF16), wsd_bf[...])
    base_ref[...] = xf + mod_ref[0, 5:6, :] * shared


def _shared(x1, norm2_g, mod3, ws_gate, ws_up, ws_down):
    tm = DISP_TM
    rows_per_batch = SEQ // tm
    row_tile = pl.BlockSpec((tm, D_MODEL), lambda i: (i, 0))
    return pl.pallas_call(
        _shared_kernel,
        grid=(N_TOK // tm,),
        in_specs=[
            row_tile,
            pl.BlockSpec((1, D_MODEL), lambda i: (0, 0)),
            pl.BlockSpec((1, 6, D_MODEL), lambda i: (i // rows_per_batch, 0, 0)),
            pl.BlockSpec((None, D_MODEL, SHARED_DIM), lambda i: (0, 0, 0)),
            pl.BlockSpec((None, D_MODEL, SHARED_DIM), lambda i: (0, 0, 0)),
            pl.BlockSpec((None, SHARED_DIM, D_MODEL), lambda i: (0, 0, 0)),
        ],
        out_specs=[pl.BlockSpec((tm, PACKED), lambda i: (i, 0)), row_tile],
        out_shape=[jax.ShapeDtypeStruct((N_TOK, PACKED), U32),
                   jax.ShapeDtypeStruct((N_TOK, D_MODEL), F32)],
        scratch_shapes=[
            pltpu.VMEM((D_MODEL, SHARED_DIM), BF16),
            pltpu.VMEM((D_MODEL, SHARED_DIM), BF16),
            pltpu.VMEM((SHARED_DIM, D_MODEL), BF16),
        ],
        compiler_params=_params(("arbitrary",)),
        name="shared",
    )(x1, norm2_g, mod3, ws_gate, ws_up, ws_down)


DIGIT_BITS = 7
ROW_BITS = 7


def _invperm_kernel(start_ref, idx_ref, pos_ref, tok_ref, dst_ref, acc_ref):
    tm = ROUTE_TM
    step = pl.program_id(0)

    @pl.when(step == 0)
    def _():
        acc_ref[...] = jnp.zeros_like(acc_ref)

    eidx = lax.broadcasted_iota(I32, (N_EXPERTS, tm), 0)
    bidx = lax.broadcasted_iota(I32, (N_ROW_BLOCKS, tm), 0)
    ridx = lax.broadcasted_iota(I32, (MOE_ROWS, tm), 0)
    tok = step * tm + lax.broadcasted_iota(I32, (1, tm), 1)
    start = start_ref[:, 0:1]
    n_digits = -(-(TOP_K * N_TOK).bit_length() // DIGIT_BITS)

    acc = acc_ref[...]
    for kk in range(TOP_K):
        first = jnp.sum(jnp.where(eidx == idx_ref[kk:kk + 1, :], start, 0.0), axis=0, keepdims=True)
        slot = first.astype(I32) + pos_ref[kk:kk + 1, :]
        in_blk = bidx == (slot >> ROW_BITS)
        in_row = jnp.where(ridx == (slot & (MOE_ROWS - 1)), 1.0, 0.0).astype(BF16)
        code = kk * N_TOK + tok + 1
        for d in range(n_digits):
            digit = ((code >> (d * DIGIT_BITS)) & ((1 << DIGIT_BITS) - 1)).astype(F32)
            part = _dot_nt(jnp.where(in_blk, digit, 0.0).astype(BF16), in_row)
            acc = acc + float(1 << (d * DIGIT_BITS)) * part
    acc_ref[...] = acc
    dst = acc.astype(I32) - 1
    dst_ref[...] = dst
    tok_ref[...] = dst & (N_TOK - 1)


def _invperm(start_col, idx_t, pos_t):
    tm = ROUTE_TM
    tok_tile = pl.BlockSpec((TOP_K, tm), lambda i: (0, i))
    table = pl.BlockSpec((N_ROW_BLOCKS, MOE_ROWS), lambda i: (0, 0))
    table_shape = jax.ShapeDtypeStruct((N_ROW_BLOCKS, MOE_ROWS), I32)
    return pl.pallas_call(
        _invperm_kernel,
        grid=(N_TOK // tm,),
        in_specs=[pl.BlockSpec((N_EXPERTS, LANES), lambda i: (0, 0)), tok_tile, tok_tile],
        out_specs=[table, table],
        out_shape=[table_shape, table_shape],
        scratch_shapes=[pltpu.VMEM((N_ROW_BLOCKS, MOE_ROWS), F32)],
        compiler_params=_params(("arbitrary",)),
        name="invperm",
    )(start_col, idx_t, pos_t)


def _weight_copies(hbm_refs, expert, bufs, slot, sems):
    return [pltpu.make_async_copy(h.at[0, expert], b.at[slot], sems.at[slot, j])
            for j, (h, b) in enumerate(zip(hbm_refs, bufs))]


def _for_each_row(fn):
    def tile(r8, carry):
        r0 = pl.multiple_of(r8 * SUBLANES, SUBLANES)
        for j in range(SUBLANES):
            fn(r0 + j)
        return carry

    lax.fori_loop(0, MOE_ROWS // SUBLANES, tile, 0)


def _experts_kernel(blk_ref, exp_ref, lo_ref, hi_ref, first_ref, par_ref, nxt_ref, n_ref,
                    tok_ref, tok_next_ref, dst_ref, dst_prev_ref, hf_hbm, wg_hbm, wu_hbm, wd_hbm, ys_hbm,
                    x_buf, o_buf, wg_buf, wu_buf, wd_buf, wg_bf, wu_bf, wd_bf, sems, gsem, ssem):
    i = pl.program_id(0)
    hbm = (wg_hbm, wu_hbm, wd_hbm)
    bufs = (wg_buf, wu_buf, wd_buf)
    n_items = n_ref[0]
    active = i < n_items
    expert = exp_ref[i]
    slot = par_ref[i]
    blk = blk_ref[i]
    bslot = blk & 1
    lo = lo_ref[i]
    hi = hi_ref[i]

    def gather_copy(tok, r, s):
        return pltpu.make_async_copy(hf_hbm.at[pl.ds(tok, 1), :], x_buf.at[s, pl.ds(r, 1), :], gsem.at[s])

    def scatter_copy(r, dst, s):
        return pltpu.make_async_copy(o_buf.at[s, pl.ds(r, 1), :], ys_hbm.at[pl.ds(dst, 1), :], ssem.at[s])

    def start_gather(table, s):
        _for_each_row(lambda r: gather_copy(table[0, 0, r], r, s).start())

    def wait_gather(s):
        _for_each_row(lambda r: gather_copy(0, 0, s).wait())

    def wait_scatter(s):
        _for_each_row(lambda r: scatter_copy(0, 0, s).wait())

    @pl.when(i == 0)
    def _():
        for cp in _weight_copies(hbm, expert, bufs, 0, sems):
            cp.start()
        start_gather(tok_ref, bslot)

    first_of_block = jnp.logical_and(active, lo == 0)
    interior = jnp.logical_and(blk >= 1, blk + 1 < N_ROW_BLOCKS)

    @pl.when(first_of_block)
    def _():
        wait_gather(bslot)

        @pl.when(blk >= 2)
        def _():
            wait_scatter(bslot)

        @pl.when(blk == 0)
        def _():
            start_gather(tok_next_ref, 1 - bslot)

        @pl.when(blk == N_ROW_BLOCKS - 1)
        def _():
            _for_each_row(lambda r: scatter_copy(r, dst_prev_ref[0, 0, r], 1 - bslot).start())

    @pl.when(jnp.logical_and(active, first_ref[i] == 1))
    def _():
        for cp in _weight_copies(hbm, expert, bufs, slot, sems):
            cp.wait()
        nxt = nxt_ref[i]

        @pl.when(nxt >= 0)
        def _():
            for cp in _weight_copies(hbm, nxt, bufs, 1 - slot, sems):
                cp.start(priority=1)

        wg_bf[...] = wg_buf[slot].astype(BF16)
        wu_bf[...] = wu_buf[slot].astype(BF16)
        wd_bf[...] = wd_buf[slot].astype(BF16)

    def ffn(first_owner):
        x = jnp.concatenate(_unpack_row_halves(x_buf[bslot]), axis=1).astype(BF16)
        hg = _dot(x, wg_bf[...])
        hu = _dot(x, wu_bf[...])
        hid = (hg * jax.nn.sigmoid(hg)) * hu
        y = _pack_row_halves(_dot(hid.astype(BF16), wd_bf[...]))
        r = lax.broadcasted_iota(I32, (MOE_ROWS, 1), 0)
        mine = (r >= lo) & (r < hi)
        o_buf[bslot] = jnp.where(mine, y, jnp.uint32(0) if first_owner else o_buf[bslot])

    @pl.when(jnp.logical_and(first_of_block, interior))
    def _():
        for r in range(MOE_ROWS):
            scatter_copy(r, dst_prev_ref[0, 0, r], 1 - bslot).start()
            gather_copy(tok_next_ref[0, 0, r], r, 1 - bslot).start()
        ffn(True)

    @pl.when(jnp.logical_and(first_of_block, jnp.logical_not(interior)))
    def _():
        ffn(True)

    @pl.when(jnp.logical_and(active, lo > 0))
    def _():
        ffn(False)

    @pl.when(i == n_items - 1)
    def _():
        _for_each_row(lambda r: scatter_copy(r, dst_ref[0, 0, r], bslot).start())
        wait_scatter(1 - bslot)
        wait_scatter(bslot)


def _experts(items, tok, dst, hf, w_gate, w_up, w_down):
    table = lambda step: pl.BlockSpec(
        (1, 1, MOE_ROWS), lambda i, blk, *_: (jnp.clip(blk[i] + step, 0, N_ROW_BLOCKS - 1), 0, 0),
        memory_space=pltpu.SMEM)
    hbm = pl.BlockSpec(memory_space=pl.ANY)
    grid_spec = pltpu.PrefetchScalarGridSpec(
        num_scalar_prefetch=len(items),
        grid=(MAX_ITEMS,),
        in_specs=[table(0), table(1), table(0), table(-1), hbm, hbm, hbm, hbm],
        out_specs=hbm,
        scratch_shapes=[
            pltpu.VMEM((2, MOE_ROWS, PACKED), U32),
            pltpu.VMEM((2, MOE_ROWS, PACKED), U32),
            pltpu.VMEM((2, D_MODEL, EXPERT_DIM), F32),
            pltpu.VMEM((2, D_MODEL, EXPERT_DIM), F32),
            pltpu.VMEM((2, EXPERT_DIM, D_MODEL), F32),
            pltpu.VMEM((D_MODEL, EXPERT_DIM), BF16),
            pltpu.VMEM((D_MODEL, EXPERT_DIM), BF16),
            pltpu.VMEM((EXPERT_DIM, D_MODEL), BF16),
            pltpu.SemaphoreType.DMA((2, 3)),
            pltpu.SemaphoreType.DMA((2,)),
            pltpu.SemaphoreType.DMA((2,)),
        ],
    )
    tok3 = tok.reshape(N_ROW_BLOCKS, 1, MOE_ROWS)
    dst3 = dst.reshape(N_ROW_BLOCKS, 1, MOE_ROWS)
    return pl.pallas_call(
        _experts_kernel,
        grid_spec=grid_spec,
        out_shape=jax.ShapeDtypeStruct((N_SLOTS, PACKED), U32),
        compiler_params=_params(("arbitrary",)),
        name="experts",
    )(*items, tok3, tok3, dst3, dst3, hf, w_gate, w_up, w_down)


def _combine_kernel(wt_ref, base_ref, mod_ref, ys_ref, o_ref):
    w_tok = wt_ref[...].T
    acc_lo = acc_hi = None
    for kk in range(TOP_K):
        lo, hi = _unpack_row_halves(ys_ref[kk])
        w = w_tok[:, kk:kk + 1]
        acc_lo = w * lo if kk == 0 else acc_lo + w * lo
        acc_hi = w * hi if kk == 0 else acc_hi + w * hi
    o_ref[:, :PACKED] = base_ref[:, :PACKED] + mod_ref[0, 5:6, :PACKED] * acc_lo
    o_ref[:, PACKED:] = base_ref[:, PACKED:] + mod_ref[0, 5:6, PACKED:] * acc_hi


def _combine(w_t, base, mod3, ys):
    tm = COMB_TM
    rows_per_batch = SEQ // tm
    return pl.pallas_call(
        _combine_kernel,
        grid=(N_TOK // tm,),
        in_specs=[
            pl.BlockSpec((TOP_K, tm), lambda i: (0, i)),
            pl.BlockSpec((tm, D_MODEL), lambda i: (i, 0)),
            pl.BlockSpec((1, 6, D_MODEL), lambda i: (i // rows_per_batch, 0, 0)),
            pl.BlockSpec((TOP_K, tm, PACKED), lambda i: (0, i, 0)),
        ],
        out_specs=pl.BlockSpec((tm, D_MODEL), lambda i: (i, 0)),
        out_shape=jax.ShapeDtypeStruct((N_TOK, D_MODEL), F32),
        compiler_params=_params(("arbitrary",)),
        name="combine",
    )(w_t, base, mod3, ys.reshape(TOP_K, N_TOK, PACKED))


def _work_items(counts):
    rows = MOE_ROWS
    grp_end = jnp.cumsum(counts)
    grp_start = grp_end - counts
    first_blk = grp_start // rows
    last_blk = (grp_end - 1) // rows
    n_blk = jnp.where(counts > 0, last_blk - first_blk + 1, 0)
    item_end = jnp.cumsum(n_blk)
    item_start = item_end - n_blk
    n_items = item_end[-1]

    ids = jnp.arange(N_EXPERTS, dtype=I32)
    nonempty = (n_blk > 0).astype(I32)
    ordinal = jnp.cumsum(nonempty) - nonempty
    later = lax.cummin(jnp.where(nonempty > 0, ids, N_EXPERTS), axis=0, reverse=True)
    nxt_e = jnp.concatenate([later[1:], jnp.full((1,), N_EXPERTS, I32)])
    nxt_e = jnp.where(nxt_e >= N_EXPERTS, -1, nxt_e)

    i = jnp.minimum(jnp.arange(MAX_ITEMS, dtype=I32), n_items - 1)
    own = (item_start[None, :] <= i[:, None]) & (i[:, None] < item_end[None, :])
    pick = lambda v: jnp.sum(jnp.where(own, v[None, :], 0), axis=1).astype(I32)
    blk = pick(first_blk - item_start) + i
    lo = jnp.maximum(pick(grp_start), blk * rows) - blk * rows
    hi = jnp.minimum(pick(grp_end), (blk + 1) * rows) - blk * rows
    first = (pick(item_start) == i).astype(I32)
    items = (blk, pick(ids), lo, hi, first, pick(ordinal) % 2, pick(nxt_e),
             n_items.astype(I32).reshape(1))
    return grp_start.astype(I32), items


def kernel(x, c, w_ada, b_ada, norm1_g, w_in, q_norm_g, k_norm_g, conv_w, attn_out_g, conv_out_g,
           w_o, norm2_g, w_router, router_bias, w_gate, w_up, w_down, ws_gate, ws_up, ws_down):
    b, s, d = x.shape
    x2 = x.reshape(b * s, d)

    c8 = jnp.pad(c, ((0, 8 - b), (0, 0)))
    mod3 = _adaln(c8, w_ada, b_ada)[:b].reshape(b, 6, d)

    qk_gain = jnp.concatenate([jnp.tile(q_norm_g[0] * HEAD_DIM ** -0.5, ATTN_HEADS),
                               jnp.tile(k_norm_g[0], ATTN_HEADS)]).reshape(1, 2 * ATTN_WIDTH)
    proj = _in_proj(x2, mod3, norm1_g, w_in, qk_gain)

    a, y = _mixer(proj, conv_w, attn_out_g, conv_out_g)
    x1 = _out_proj(a, y, w_o, x2, mod3)

    w_rt = w_router[0].T
    bias_col = jnp.broadcast_to(router_bias[0][:, None], (N_EXPERTS, LANES))
    idx_t, w_t, pos_t, cnt = _route(x1, norm2_g, mod3, w_rt, bias_col)

    grp_start, items = _work_items(cnt[:, 0])
    start_col = jnp.broadcast_to(grp_start.astype(F32)[:, None], (N_EXPERTS, LANES))
    slot_tok, slot_dst = _invperm(start_col, idx_t, pos_t)
    hf, base = _shared(x1, norm2_g, mod3, ws_gate, ws_up, ws_down)
    ys = _experts(items, slot_tok, slot_dst, hf, w_gate, w_up, w_down)
    out = _combine(w_t, base, mod3, ys)
    return out.reshape(b, s, d)
```

```python
import functools

import jax
import jax.numpy as jnp
from jax import lax
from jax.experimental import pallas as pl
from jax.experimental.pallas import tpu as pltpu

F32 = jnp.float32
BF16 = jnp.bfloat16
I32 = jnp.int32
U32 = jnp.uint32

D_MODEL = 2048
BATCH = 4
SEQ = 2048
N_TOK = BATCH * SEQ
HEAD_DIM = 128
ATTN_WIDTH = 1024
ATTN_HEADS = 8
CONV_WIDTH = 1024
CONV_GROUPS = 8
IN_WIDTH = 6144
CONV_K = 3
MOBA_BLOCK = 256
MOBA_NB = SEQ // MOBA_BLOCK
MOBA_TOPK = 3
N_EXPERTS = 256
TOP_K = 8
N_GROUPS = 8
GROUP_SIZE = N_EXPERTS // N_GROUPS
TOPK_GROUPS = 4
EXPERT_DIM = 512
SHARED_DIM = 512
ROUTED_SCALE = 2.5
EPS = 1e-6

LANES = 128
SUBLANES = 8
MXU_COLS = 256
MASKED = -1e30
VMEM_LIMIT = 56 * 1024 * 1024

ADA_TN = 1024
PROJ_TM = 1024
PROJ_TN = 512
ROUTE_TM = 256
DISP_TM = 256
MOE_ROWS = 128
WEIGHT_SLOTS = 3
N_SLOTS = N_TOK * TOP_K
N_ROW_BLOCKS = N_SLOTS // MOE_ROWS
MAX_ITEMS = N_ROW_BLOCKS + N_EXPERTS
COMB_TM = 256


def _dot(a, b):
    return jnp.dot(a, b, preferred_element_type=F32)


def _dot_nt(a, b):
    return lax.dot_general(a, b, (((1,), (1,)), ((), ())), preferred_element_type=F32)


def _params(sem):
    return pltpu.CompilerParams(dimension_semantics=sem, vmem_limit_bytes=VMEM_LIMIT)


PACKED = D_MODEL // 2


def _pack_row_halves(v):
    lo = pltpu.bitcast(v[:, :PACKED].astype(BF16).astype(F32), U32)
    hi = pltpu.bitcast(v[:, PACKED:].astype(BF16).astype(F32), U32)
    return hi | (lo >> 16)


def _unpack_row_halves(w):
    lo = pltpu.bitcast(w << 16, F32)
    hi = pltpu.bitcast(w & jnp.uint32(0xFFFF0000), F32)
    return lo, hi


def _adaln_kernel(c_ref, w_ref, b_ref, o_ref):
    c = c_ref[...]
    o_ref[...] = _dot(c * jax.nn.sigmoid(c), w_ref[...]) + b_ref[...]


def _adaln(c8, w_ada, b_ada):
    n = w_ada.shape[-1]
    return pl.pallas_call(
        _adaln_kernel,
        grid=(n // ADA_TN,),
        in_specs=[
            pl.BlockSpec((8, D_MODEL), lambda j: (0, 0)),
            pl.BlockSpec((None, D_MODEL, ADA_TN), lambda j: (0, 0, j)),
            pl.BlockSpec((1, ADA_TN), lambda j: (0, j)),
        ],
        out_specs=pl.BlockSpec((8, ADA_TN), lambda j: (0, j)),
        out_shape=jax.ShapeDtypeStruct((8, n), F32),
        compiler_params=_params(("arbitrary",)),
        name="adaln",
    )(c8, w_ada, b_ada)


def _in_proj_kernel(x_ref, mod_ref, g_ref, w_ref, qkg_ref, o_ref, h_ref):
    j = pl.program_id(1)

    @pl.when(j == 0)
    def _():
        xf = x_ref[...]
        ms = jnp.mean(xf * xf, axis=-1, keepdims=True)
        y = xf * lax.rsqrt(ms + EPS) * g_ref[...]
        h_ref[...] = (y * (1.0 + mod_ref[0, 1:2, :]) + mod_ref[0, 0:1, :]).astype(BF16)

    @pl.when(j < 2 * ATTN_WIDTH // PROJ_TN)
    def _():
        for pair in range(PROJ_TN // MXU_COLS):
            cols = slice(pair * MXU_COLS, (pair + 1) * MXU_COLS)
            acc = _dot(h_ref[...], w_ref[:, cols].astype(BF16))
            for hh in range(MXU_COLS // HEAD_DIM):
                a = acc[:, hh * HEAD_DIM:(hh + 1) * HEAD_DIM]
                sl = slice(pair * MXU_COLS + hh * HEAD_DIM, pair * MXU_COLS + (hh + 1) * HEAD_DIM)
                ms = jnp.mean(a * a, axis=-1, keepdims=True)
                o_ref[:, sl] = (a * lax.rsqrt(ms + EPS) * qkg_ref[:, sl]).astype(BF16)

    @pl.when(j >= 2 * ATTN_WIDTH // PROJ_TN)
    def _():
        o_ref[...] = _dot(h_ref[...], w_ref[...].astype(BF16)).astype(BF16)


def _in_proj(x2, mod3, norm1_g, w_in, qk_gain):
    n_qk = 2 * ATTN_WIDTH // PROJ_TN
    rows_per_batch = SEQ // PROJ_TM
    return pl.pallas_call(
        _in_proj_kernel,
        grid=(N_TOK // PROJ_TM, IN_WIDTH // PROJ_TN),
        in_specs=[
            pl.BlockSpec((PROJ_TM, D_MODEL), lambda i, j: (i, 0)),
            pl.BlockSpec((1, 6, D_MODEL), lambda i, j: (i // rows_per_batch, 0, 0)),
            pl.BlockSpec((1, D_MODEL), lambda i, j: (0, 0)),
            pl.BlockSpec((None, D_MODEL, PROJ_TN), lambda i, j: (0, 0, j)),
            pl.BlockSpec((1, PROJ_TN), lambda i, j: (0, jnp.minimum(j, n_qk - 1))),
        ],
        out_specs=pl.BlockSpec((PROJ_TM, PROJ_TN), lambda i, j: (i, j)),
        out_shape=jax.ShapeDtypeStruct((N_TOK, IN_WIDTH), BF16),
        scratch_shapes=[pltpu.VMEM((PROJ_TM, D_MODEL), BF16)],
        compiler_params=_params(("arbitrary", "arbitrary")),
        name="in_proj",
    )(x2, mod3, norm1_g, w_in, qk_gain)


def _mixer_kernel(q_ref, k_ref, v_ref, u_ref, bg_ref, cg_ref, cw_ref, ag_ref, yg_ref,
                  a_ref, y_ref, s_ref):
    blk = MOBA_BLOCK
    k = k_ref[...]
    v = v_ref[...]

    km = jnp.mean(k.astype(F32).reshape(MOBA_NB, blk, HEAD_DIM), axis=1)
    km_hi = km.astype(BF16)
    km_lo = (km - km_hi.astype(F32)).astype(BF16)
    zpad = jnp.zeros((LANES - MOBA_NB, HEAD_DIM), BF16)
    km_hi = jnp.concatenate([km_hi, zpad], axis=0)
    km_lo = jnp.concatenate([km_lo, zpad], axis=0)

    lane = lax.broadcasted_iota(I32, (blk, LANES), 1)
    row = lax.broadcasted_iota(I32, (blk, blk), 0)
    col = lax.broadcasted_iota(I32, (blk, blk), 1)

    for i in range(MOBA_NB):
        qi = q_ref[i * blk:(i + 1) * blk, :]
        if i > 0:
            if i > MOBA_TOPK:
                g = _dot_nt(qi, km_hi) + _dot_nt(qi, km_lo)
                rank = jnp.zeros((blk, LANES), F32)
                for jp in range(i):
                    cj = g[:, jp:jp + 1]
                    tie = jnp.where(lane > jp, 1.0, 0.0)
                    rank = rank + jnp.where(cj > g, 1.0, jnp.where(cj == g, tie, 0.0))
                sel = jnp.where(rank < MOBA_TOPK, 1.0, 0.0)
            else:
                sel = jnp.ones((blk, LANES), F32)
            for j in range(i):
                s = _dot_nt(qi, k[j * blk:(j + 1) * blk])
                s_ref[:, j * blk:(j + 1) * blk] = jnp.where(sel[:, j:j + 1] > 0.5, s, MASKED)
        s = _dot_nt(qi, k[i * blk:(i + 1) * blk])
        s_ref[:, i * blk:(i + 1) * blk] = jnp.where(col <= row, s, MASKED)

        width = (i + 1) * blk
        sc = s_ref[:, :width]
        m = jnp.max(sc, axis=-1, keepdims=True)
        p = jnp.exp(sc - m)
        denom = jnp.sum(p, axis=-1, keepdims=True)
        o = _dot(p.astype(BF16), v[:width]) / denom
        ms = jnp.mean(o * o, axis=-1, keepdims=True)
        a_ref[i * blk:(i + 1) * blk, :] = (o * lax.rsqrt(ms + EPS) * ag_ref[...]).astype(BF16)

    z = cg_ref[...].astype(F32) * u_ref[...].astype(F32)
    t = lax.broadcasted_iota(I32, z.shape, 0)
    z1 = jnp.where(t >= 1, pltpu.roll(z, 1, 0), 0.0)
    z2 = jnp.where(t >= 2, pltpu.roll(z, 2, 0), 0.0)
    y = cw_ref[0:1, :] * z2 + cw_ref[1:2, :] * z1 + cw_ref[2:3, :] * z
    y = bg_ref[...].astype(F32) * y
    ms = jnp.mean(y * y, axis=-1, keepdims=True)
    y_ref[...] = (y * lax.rsqrt(ms + EPS) * yg_ref[...]).astype(BF16)


def _mixer(proj, conv_w, attn_out_g, conv_out_g):
    h = ATTN_HEADS

    def col(off):
        return pl.BlockSpec((SEQ, HEAD_DIM), lambda b, g: (b, off + g))

    out_spec = pl.BlockSpec((SEQ, HEAD_DIM), lambda b, g: (b, g))
    vec_spec = pl.BlockSpec((1, HEAD_DIM), lambda b, g: (0, g))
    return pl.pallas_call(
        _mixer_kernel,
        grid=(BATCH, h),
        in_specs=[col(0), col(h), col(2 * h), col(3 * h), col(4 * h), col(5 * h),
                  pl.BlockSpec((None, CONV_K, HEAD_DIM), lambda b, g: (0, 0, g)),
                  vec_spec, vec_spec],
        out_specs=[out_spec, out_spec],
        out_shape=[jax.ShapeDtypeStruct((N_TOK, ATTN_WIDTH), BF16),
                   jax.ShapeDtypeStruct((N_TOK, CONV_WIDTH), BF16)],
        scratch_shapes=[pltpu.VMEM((MOBA_BLOCK, SEQ), F32)],
        compiler_params=_params(("arbitrary", "arbitrary")),
        name="mixer",
    )(proj, proj, proj, proj, proj, proj, conv_w, attn_out_g, conv_out_g)


def _out_proj_kernel(a_ref, y_ref, wa_ref, wb_ref, x_ref, mod_ref, o_ref):
    mix = _dot(a_ref[...], wa_ref[...].astype(BF16)) + _dot(y_ref[...], wb_ref[...].astype(BF16))
    o_ref[...] = x_ref[...] + mod_ref[0, 2:3, :] * mix


def _out_proj(a, y, w_o, x2, mod3):
    rows_per_batch = SEQ // PROJ_TM
    return pl.pallas_call(
        _out_proj_kernel,
        grid=(N_TOK // PROJ_TM, D_MODEL // PROJ_TN),
        in_specs=[
            pl.BlockSpec((PROJ_TM, ATTN_WIDTH), lambda i, j: (i, 0)),
            pl.BlockSpec((PROJ_TM, CONV_WIDTH), lambda i, j: (i, 0)),
            pl.BlockSpec((None, ATTN_WIDTH, PROJ_TN), lambda i, j: (0, 0, j)),
            pl.BlockSpec((None, CONV_WIDTH, PROJ_TN), lambda i, j: (0, 1, j)),
            pl.BlockSpec((PROJ_TM, PROJ_TN), lambda i, j: (i, j)),
            pl.BlockSpec((1, 6, PROJ_TN), lambda i, j: (i // rows_per_batch, 0, j)),
        ],
        out_specs=pl.BlockSpec((PROJ_TM, PROJ_TN), lambda i, j: (i, j)),
        out_shape=jax.ShapeDtypeStruct((N_TOK, D_MODEL), F32),
        compiler_params=_params(("arbitrary", "arbitrary")),
        name="out_proj",
    )(a, y, w_o, w_o, x2, mod3)


def _norm2_modulate(xf, g_ref, mod_ref):
    ms = jnp.mean(xf * xf, axis=-1, keepdims=True)
    return xf * lax.rsqrt(ms + EPS) * g_ref[...] * (1.0 + mod_ref[0, 4:5, :]) + mod_ref[0, 3:4, :]


def _route_kernel(x_ref, g_ref, mod_ref, wrt_ref, bias_ref,
                  idx_ref, wt_ref, pos_ref, cnt_ref, carry_ref):
    tm = ROUTE_TM
    ne = N_EXPERTS

    @pl.when(pl.program_id(0) == 0)
    def _():
        carry_ref[...] = jnp.zeros_like(carry_ref)

    hb = _norm2_modulate(x_ref[...], g_ref, mod_ref).astype(BF16)

    scores = jax.nn.sigmoid(_dot_nt(wrt_ref[...].astype(BF16), hb))
    choice = scores + bias_ref[:, 0:1]
    ninf = -jnp.inf

    gi = lax.broadcasted_iota(I32, (GROUP_SIZE, tm), 0)
    rows = []
    for g in range(N_GROUPS):
        blk = choice[g * GROUP_SIZE:(g + 1) * GROUP_SIZE, :]
        m1 = jnp.max(blk, axis=0, keepdims=True)
        i1 = jnp.min(jnp.where(blk == m1, gi, GROUP_SIZE), axis=0, keepdims=True)
        m2 = jnp.max(jnp.where(gi == i1, ninf, blk), axis=0, keepdims=True)
        rows.append(m1 + m2)
    gsc = jnp.concatenate(rows, axis=0)

    gidx = lax.broadcasted_iota(I32, (N_GROUPS, tm), 0)
    rank = jnp.zeros((N_GROUPS, tm), F32)
    for gp in range(N_GROUPS):
        r = gsc[gp:gp + 1, :]
        tie = jnp.where(gidx > gp, 1.0, 0.0)
        rank = rank + jnp.where(r > gsc, 1.0, jnp.where(r == gsc, tie, 0.0))
    gsel = jnp.where(rank < TOPK_GROUPS, 1.0, 0.0)
    esel = jnp.concatenate(
        [jnp.broadcast_to(gsel[g:g + 1, :], (GROUP_SIZE, tm)) for g in range(N_GROUPS)], axis=0)
    masked = jnp.where(esel > 0.5, choice, ninf)

    eidx = lax.broadcasted_iota(I32, (ne, tm), 0)
    idx_rows, w_rows = [], []
    onehot = jnp.zeros((ne, tm), F32)
    for _ in range(TOP_K):
        m = jnp.max(masked, axis=0, keepdims=True)
        sel = jnp.min(jnp.where(masked == m, eidx, ne), axis=0, keepdims=True)
        hit = eidx == sel
        idx_rows.append(sel)
        w_rows.append(jnp.sum(jnp.where(hit, scores, 0.0), axis=0, keepdims=True))
        masked = jnp.where(hit, ninf, masked)
        onehot = jnp.where(hit, 1.0, onehot)
    wsel = jnp.concatenate(w_rows, axis=0)
    idx_ref[...] = jnp.concatenate(idx_rows, axis=0)
    wt_ref[...] = wsel / jnp.sum(wsel, axis=0, keepdims=True) * ROUTED_SCALE

    ti = lax.broadcasted_iota(I32, (tm, tm), 0)
    tj = lax.broadcasted_iota(I32, (tm, tm), 1)
    upper = jnp.where(ti < tj, 1.0, 0.0).astype(BF16)
    before = _dot(onehot.astype(BF16), upper) + carry_ref[:, 0:1]
    pos_rows = [jnp.sum(jnp.where(eidx == idx_rows[kk], before, 0.0), axis=0, keepdims=True)
                for kk in range(TOP_K)]
    pos_ref[...] = jnp.concatenate(pos_rows, axis=0).astype(I32)
    carry_ref[...] = carry_ref[...] + jnp.sum(onehot, axis=1, keepdims=True)
    cnt_ref[...] = carry_ref[...].astype(I32)


def _route(x1, norm2_g, mod3, w_rt, bias_col):
    tm = ROUTE_TM
    rows_per_batch = SEQ // tm
    full = lambda shape: pl.BlockSpec(shape, lambda i: tuple(0 for _ in shape))
    tok_tile = pl.BlockSpec((TOP_K, tm), lambda i: (0, i))
    return pl.pallas_call(
        _route_kernel,
        grid=(N_TOK // tm,),
        in_specs=[
            pl.BlockSpec((tm, D_MODEL), lambda i: (i, 0)),
            full((1, D_MODEL)),
            pl.BlockSpec((1, 6, D_MODEL), lambda i: (i // rows_per_batch, 0, 0)),
            full((N_EXPERTS, D_MODEL)),
            full((N_EXPERTS, LANES)),
        ],
        out_specs=[tok_tile, tok_tile, tok_tile, full((N_EXPERTS, LANES))],
        out_shape=[
            jax.ShapeDtypeStruct((TOP_K, N_TOK), I32),
            jax.ShapeDtypeStruct((TOP_K, N_TOK), F32),
            jax.ShapeDtypeStruct((TOP_K, N_TOK), I32),
            jax.ShapeDtypeStruct((N_EXPERTS, LANES), I32),
        ],
        scratch_shapes=[pltpu.VMEM((N_EXPERTS, LANES), F32)],
        compiler_params=_params(("arbitrary",)),
        name="route",
    )(x1, norm2_g, mod3, w_rt, bias_col)


def _shared_kernel(x_ref, g_ref, mod_ref, wsg_ref, wsu_ref, wsd_ref, hf_ref, base_ref, wsg_bf, wsu_bf, wsd_bf):
    @pl.when(pl.program_id(0) == 0)
    def _():
        wsg_bf[...] = wsg_ref[...].astype(BF16)
        wsu_bf[...] = wsu_ref[...].astype(BF16)
        wsd_bf[...] = wsd_ref[...].astype(BF16)

    xf = x_ref[...]
    hf = _norm2_modulate(xf, g_ref, mod_ref)
    hf_ref[...] = _pack_row_halves(hf)

    hb = hf.astype(BF16)
    hg = _dot(hb, wsg_bf[...])
    hu = _dot(hb, wsu_bf[...])
    hid = (hg * jax.nn.sigmoid(hg)) * hu
    shared = _dot(hid.astype(BF16), wsd_bf[...])
    base_ref[...] = xf + mod_ref[0, 5:6, :] * shared


def _shared(x1, norm2_g, mod3, ws_gate, ws_up, ws_down):
    tm = DISP_TM
    rows_per_batch = SEQ // tm
    row_tile = pl.BlockSpec((tm, D_MODEL), lambda i: (i, 0))
    return pl.pallas_call(
        _shared_kernel,
        grid=(N_TOK // tm,),
        in_specs=[
            row_tile,
            pl.BlockSpec((1, D_MODEL), lambda i: (0, 0)),
            pl.BlockSpec((1, 6, D_MODEL), lambda i: (i // rows_per_batch, 0, 0)),
            pl.BlockSpec((None, D_MODEL, SHARED_DIM), lambda i: (0, 0, 0)),
            pl.BlockSpec((None, D_MODEL, SHARED_DIM), lambda i: (0, 0, 0)),
            pl.BlockSpec((None, SHARED_DIM, D_MODEL), lambda i: (0, 0, 0)),
        ],
        out_specs=[pl.BlockSpec((tm, PACKED), lambda i: (i, 0)), row_tile],
        out_shape=[jax.ShapeDtypeStruct((N_TOK, PACKED), U32),
                   jax.ShapeDtypeStruct((N_TOK, D_MODEL), F32)],
        scratch_shapes=[
            pltpu.VMEM((D_MODEL, SHARED_DIM), BF16),
            pltpu.VMEM((D_MODEL, SHARED_DIM), BF16),
            pltpu.VMEM((SHARED_DIM, D_MODEL), BF16),
        ],
        compiler_params=_params(("arbitrary",)),
        name="shared",
    )(x1, norm2_g, mod3, ws_gate, ws_up, ws_down)


DIGIT_BITS = 7
ROW_BITS = 7


def _invperm_kernel(start_ref, idx_ref, pos_ref, tok_ref, dst_ref, acc_ref):
    tm = ROUTE_TM
    step = pl.program_id(0)

    @pl.when(step == 0)
    def _():
        acc_ref[...] = jnp.zeros_like(acc_ref)

    eidx = lax.broadcasted_iota(I32, (N_EXPERTS, tm), 0)
    bidx = lax.broadcasted_iota(I32, (N_ROW_BLOCKS, tm), 0)
    ridx = lax.broadcasted_iota(I32, (MOE_ROWS, tm), 0)
    tok = step * tm + lax.broadcasted_iota(I32, (1, tm), 1)
    start = start_ref[:, 0:1]
    n_digits = -(-(TOP_K * N_TOK).bit_length() // DIGIT_BITS)

    acc = acc_ref[...]
    for kk in range(TOP_K):
        first = jnp.sum(jnp.where(eidx == idx_ref[kk:kk + 1, :], start, 0.0), axis=0, keepdims=True)
        slot = first.astype(I32) + pos_ref[kk:kk + 1, :]
        in_blk = bidx == (slot >> ROW_BITS)
        in_row = jnp.where(ridx == (slot & (MOE_ROWS - 1)), 1.0, 0.0).astype(BF16)
        code = kk * N_TOK + tok + 1
        for d in range(n_digits):
            digit = ((code >> (d * DIGIT_BITS)) & ((1 << DIGIT_BITS) - 1)).astype(F32)
            part = _dot_nt(jnp.where(in_blk, digit, 0.0).astype(BF16), in_row)
            acc = acc + float(1 << (d * DIGIT_BITS)) * part
    acc_ref[...] = acc
    dst = acc.astype(I32) - 1
    dst_ref[...] = dst
    tok_ref[...] = dst & (N_TOK - 1)


def _invperm(start_col, idx_t, pos_t):
    tm = ROUTE_TM
    tok_tile = pl.BlockSpec((TOP_K, tm), lambda i: (0, i))
    table = pl.BlockSpec((N_ROW_BLOCKS, MOE_ROWS), lambda i: (0, 0))
    table_shape = jax.ShapeDtypeStruct((N_ROW_BLOCKS, MOE_ROWS), I32)
    return pl.pallas_call(
        _invperm_kernel,
        grid=(N_TOK // tm,),
        in_specs=[pl.BlockSpec((N_EXPERTS, LANES), lambda i: (0, 0)), tok_tile, tok_tile],
        out_specs=[table, table],
        out_shape=[table_shape, table_shape],
        scratch_shapes=[pltpu.VMEM((N_ROW_BLOCKS, MOE_ROWS), F32)],
        compiler_params=_params(("arbitrary",)),
        name="invperm",
    )(start_col, idx_t, pos_t)


def _weight_copies(hbm_refs, expert, bufs, slot, sems):
    return [pltpu.make_async_copy(h.at[0, expert], b.at[slot], sems.at[slot, j])
            for j, (h, b) in enumerate(zip(hbm_refs, bufs))]


def _for_each_row(fn):
    def tile(r8, carry):
        r0 = pl.multiple_of(r8 * SUBLANES, SUBLANES)
        for j in range(SUBLANES):
            fn(r0 + j)
        return carry

    lax.fori_loop(0, MOE_ROWS // SUBLANES, tile, 0)


def _experts_kernel(blk_ref, exp_ref, lo_ref, hi_ref, first_ref, par_ref, nxt_ref, nxt2_ref, n_ref,
                    tok_ref, tok_next_ref, dst_ref, dst_prev_ref, hf_hbm, wg_hbm, wu_hbm, wd_hbm, ys_hbm,
                    x_buf, o_buf, wg_buf, wu_buf, wd_buf, wg_bf, wu_bf, wd_bf, sems, gsem, ssem):
    i = pl.program_id(0)
    hbm = (wg_hbm, wu_hbm, wd_hbm)
    bufs = (wg_buf, wu_buf, wd_buf)
    n_items = n_ref[0]
    active = i < n_items
    expert = exp_ref[i]
    slot = par_ref[i]
    blk = blk_ref[i]
    bslot = blk & 1
    lo = lo_ref[i]
    hi = hi_ref[i]

    def gather_copy(tok, r, s):
        return pltpu.make_async_copy(hf_hbm.at[pl.ds(tok, 1), :], x_buf.at[s, pl.ds(r, 1), :], gsem.at[s])

    def scatter_copy(r, dst, s):
        return pltpu.make_async_copy(o_buf.at[s, pl.ds(r, 1), :], ys_hbm.at[pl.ds(dst, 1), :], ssem.at[s])

    def start_gather(table, s):
        _for_each_row(lambda r: gather_copy(table[0, 0, r], r, s).start())

    def wait_gather(s):
        _for_each_row(lambda r: gather_copy(0, 0, s).wait())

    def wait_scatter(s):
        _for_each_row(lambda r: scatter_copy(0, 0, s).wait())

    @pl.when(i == 0)
    def _():
        for cp in _weight_copies(hbm, expert, bufs, 0, sems):
            cp.start(priority=1)

        @pl.when(nxt_ref[0] >= 0)
        def _():
            for cp in _weight_copies(hbm, nxt_ref[0], bufs, 1, sems):
                cp.start(priority=1)

        start_gather(tok_ref, bslot)

    first_of_block = jnp.logical_and(active, lo == 0)
    interior = jnp.logical_and(blk >= 1, blk + 1 < N_ROW_BLOCKS)

    @pl.when(first_of_block)
    def _():
        wait_gather(bslot)

        @pl.when(blk >= 2)
        def _():
            wait_scatter(bslot)

        @pl.when(blk == 0)
        def _():
            start_gather(tok_next_ref, 1 - bslot)

        @pl.when(blk == N_ROW_BLOCKS - 1)
        def _():
            _for_each_row(lambda r: scatter_copy(r, dst_prev_ref[0, 0, r], 1 - bslot).start())

    @pl.when(jnp.logical_and(active, first_ref[i] == 1))
    def _():
        for cp in _weight_copies(hbm, expert, bufs, slot, sems):
            cp.wait()
        ahead = nxt2_ref[i]

        @pl.when(ahead >= 0)
        def _():
            for cp in _weight_copies(hbm, ahead, bufs, jnp.where(slot == 0, WEIGHT_SLOTS - 1, slot - 1), sems):
                cp.start(priority=1)

        wg_bf[...] = wg_buf[slot].astype(BF16)
        wu_bf[...] = wu_buf[slot].astype(BF16)
        wd_bf[...] = wd_buf[slot].astype(BF16)

    def ffn(first_owner):
        x = jnp.concatenate(_unpack_row_halves(x_buf[bslot]), axis=1).astype(BF16)
        hg = _dot(x, wg_bf[...])
        hu = _dot(x, wu_bf[...])
        hid = (hg * jax.nn.sigmoid(hg)) * hu
        y = _pack_row_halves(_dot(hid.astype(BF16), wd_bf[...]))
        r = lax.broadcasted_iota(I32, (MOE_ROWS, 1), 0)
        mine = (r >= lo) & (r < hi)
        o_buf[bslot] = jnp.where(mine, y, jnp.uint32(0) if first_owner else o_buf[bslot])

    @pl.when(jnp.logical_and(first_of_block, interior))
    def _():
        for r in range(MOE_ROWS):
            scatter_copy(r, dst_prev_ref[0, 0, r], 1 - bslot).start()
            gather_copy(tok_next_ref[0, 0, r], r, 1 - bslot).start()
        ffn(True)

    @pl.when(jnp.logical_and(first_of_block, jnp.logical_not(interior)))
    def _():
        ffn(True)

    @pl.when(jnp.logical_and(active, lo > 0))
    def _():
        ffn(False)

    @pl.when(i == n_items - 1)
    def _():
        _for_each_row(lambda r: scatter_copy(r, dst_ref[0, 0, r], bslot).start())
        wait_scatter(1 - bslot)
        wait_scatter(bslot)


def _experts(items, tok, dst, hf, w_gate, w_up, w_down):
    table = lambda step: pl.BlockSpec(
        (1, 1, MOE_ROWS), lambda i, blk, *_: (jnp.clip(blk[i] + step, 0, N_ROW_BLOCKS - 1), 0, 0),
        memory_space=pltpu.SMEM)
    hbm = pl.BlockSpec(memory_space=pl.ANY)
    grid_spec = pltpu.PrefetchScalarGridSpec(
        num_scalar_prefetch=len(items),
        grid=(MAX_ITEMS,),
        in_specs=[table(0), table(1), table(0), table(-1), hbm, hbm, hbm, hbm],
        out_specs=hbm,
        scratch_shapes=[
            pltpu.VMEM((2, MOE_ROWS, PACKED), U32),
            pltpu.VMEM((2, MOE_ROWS, PACKED), U32),
            pltpu.VMEM((WEIGHT_SLOTS, D_MODEL, EXPERT_DIM), F32),
            pltpu.VMEM((WEIGHT_SLOTS, D_MODEL, EXPERT_DIM), F32),
            pltpu.VMEM((WEIGHT_SLOTS, EXPERT_DIM, D_MODEL), F32),
            pltpu.VMEM((D_MODEL, EXPERT_DIM), BF16),
            pltpu.VMEM((D_MODEL, EXPERT_DIM), BF16),
            pltpu.VMEM((EXPERT_DIM, D_MODEL), BF16),
            pltpu.SemaphoreType.DMA((WEIGHT_SLOTS, 3)),
            pltpu.SemaphoreType.DMA((2,)),
            pltpu.SemaphoreType.DMA((2,)),
        ],
    )
    tok3 = tok.reshape(N_ROW_BLOCKS, 1, MOE_ROWS)
    dst3 = dst.reshape(N_ROW_BLOCKS, 1, MOE_ROWS)
    return pl.pallas_call(
        _experts_kernel,
        grid_spec=grid_spec,
        out_shape=jax.ShapeDtypeStruct((N_SLOTS, PACKED), U32),
        compiler_params=_params(("arbitrary",)),
        name="experts",
    )(*items, tok3, tok3, dst3, dst3, hf, w_gate, w_up, w_down)


def _combine_kernel(wt_ref, base_ref, mod_ref, ys_ref, o_ref):
    w_tok = wt_ref[...].T
    acc_lo = acc_hi = None
    for kk in range(TOP_K):
        lo, hi = _unpack_row_halves(ys_ref[kk])
        w = w_tok[:, kk:kk + 1]
        acc_lo = w * lo if kk == 0 else acc_lo + w * lo
        acc_hi = w * hi if kk == 0 else acc_hi + w * hi
    o_ref[:, :PACKED] = base_ref[:, :PACKED] + mod_ref[0, 5:6, :PACKED] * acc_lo
    o_ref[:, PACKED:] = base_ref[:, PACKED:] + mod_ref[0, 5:6, PACKED:] * acc_hi


def _combine(w_t, base, mod3, ys):
    tm = COMB_TM
    rows_per_batch = SEQ // tm
    return pl.pallas_call(
        _combine_kernel,
        grid=(N_TOK // tm,),
        in_specs=[
            pl.BlockSpec((TOP_K, tm), lambda i: (0, i)),
            pl.BlockSpec((tm, D_MODEL), lambda i: (i, 0)),
            pl.BlockSpec((1, 6, D_MODEL), lambda i: (i // rows_per_batch, 0, 0)),
            pl.BlockSpec((TOP_K, tm, PACKED), lambda i: (0, i, 0)),
        ],
        out_specs=pl.BlockSpec((tm, D_MODEL), lambda i: (i, 0)),
        out_shape=jax.ShapeDtypeStruct((N_TOK, D_MODEL), F32),
        compiler_params=_params(("arbitrary",)),
        name="combine",
    )(w_t, base, mod3, ys.reshape(TOP_K, N_TOK, PACKED))


def _work_items(counts):
    rows = MOE_ROWS
    grp_end = jnp.cumsum(counts)
    grp_start = grp_end - counts
    first_blk = grp_start // rows
    last_blk = (grp_end - 1) // rows
    n_blk = jnp.where(counts > 0, last_blk - first_blk + 1, 0)
    item_end = jnp.cumsum(n_blk)
    item_start = item_end - n_blk
    n_items = item_end[-1]

    ids = jnp.arange(N_EXPERTS, dtype=I32)
    nonempty = (n_blk > 0).astype(I32)
    ordinal = jnp.cumsum(nonempty) - nonempty
    later = lax.cummin(jnp.where(nonempty > 0, ids, N_EXPERTS), axis=0, reverse=True)
    nxt_e = jnp.concatenate([later[1:], jnp.full((1,), N_EXPERTS, I32)])
    nxt_e = jnp.where(nxt_e >= N_EXPERTS, -1, nxt_e)
    nxt2_e = jnp.where(nxt_e >= 0, jnp.take(nxt_e, jnp.maximum(nxt_e, 0)), -1)

    i = jnp.minimum(jnp.arange(MAX_ITEMS, dtype=I32), n_items - 1)
    own = (item_start[None, :] <= i[:, None]) & (i[:, None] < item_end[None, :])
    pick = lambda v: jnp.sum(jnp.where(own, v[None, :], 0), axis=1).astype(I32)
    blk = pick(first_blk - item_start) + i
    lo = jnp.maximum(pick(grp_start), blk * rows) - blk * rows
    hi = jnp.minimum(pick(grp_end), (blk + 1) * rows) - blk * rows
    first = (pick(item_start) == i).astype(I32)
    items = (blk, pick(ids), lo, hi, first, pick(ordinal) % WEIGHT_SLOTS, pick(nxt_e), pick(nxt2_e),
             n_items.astype(I32).reshape(1))
    return grp_start.astype(I32), items


def kernel(x, c, w_ada, b_ada, norm1_g, w_in, q_norm_g, k_norm_g, conv_w, attn_out_g, conv_out_g,
           w_o, norm2_g, w_router, router_bias, w_gate, w_up, w_down, ws_gate, ws_up, ws_down):
    b, s, d = x.shape
    x2 = x.reshape(b * s, d)

    c8 = jnp.pad(c, ((0, 8 - b), (0, 0)))
    mod3 = _adaln(c8, w_ada, b_ada)[:b].reshape(b, 6, d)

    qk_gain = jnp.concatenate([jnp.tile(q_norm_g[0] * HEAD_DIM ** -0.5, ATTN_HEADS),
                               jnp.tile(k_norm_g[0], ATTN_HEADS)]).reshape(1, 2 * ATTN_WIDTH)
    proj = _in_proj(x2, mod3, norm1_g, w_in, qk_gain)

    a, y = _mixer(proj, conv_w, attn_out_g, conv_out_g)
    x1 = _out_proj(a, y, w_o, x2, mod3)

    w_rt = w_router[0].T
    bias_col = jnp.broadcast_to(router_bias[0][:, None], (N_EXPERTS, LANES))
    idx_t, w_t, pos_t, cnt = _route(x1, norm2_g, mod3, w_rt, bias_col)

    grp_start, items = _work_items(cnt[:, 0])
    start_col = jnp.broadcast_to(grp_start.astype(F32)[:, None], (N_EXPERTS, LANES))
    slot_tok, slot_dst = _invperm(start_col, idx_t, pos_t)
    hf, base = _shared(x1, norm2_g, mod3, ws_gate, ws_up, ws_down)
    ys = _experts(items, slot_tok, slot_dst, hf, w_gate, w_up, w_down)
    out = _combine(w_t, base, mod3, ys)
    return out.reshape(b, s, d)
```

```python
import functools

import jax
import jax.numpy as jnp
from jax import lax
from jax.experimental import pallas as pl
from jax.experimental.pallas import tpu as pltpu

F32 = jnp.float32
BF16 = jnp.bfloat16
I32 = jnp.int32
U32 = jnp.uint32

D_MODEL = 2048
BATCH = 4
SEQ = 2048
N_TOK = BATCH * SEQ
HEAD_DIM = 128
ATTN_WIDTH = 1024
ATTN_HEADS = 8
CONV_WIDTH = 1024
CONV_GROUPS = 8
IN_WIDTH = 6144
CONV_K = 3
MOBA_BLOCK = 256
MOBA_NB = SEQ // MOBA_BLOCK
MOBA_TOPK = 3
N_EXPERTS = 256
TOP_K = 8
N_GROUPS = 8
GROUP_SIZE = N_EXPERTS // N_GROUPS
TOPK_GROUPS = 4
EXPERT_DIM = 512
SHARED_DIM = 512
ROUTED_SCALE = 2.5
EPS = 1e-6

LANES = 128
SUBLANES = 8
MXU_COLS = 256
MASKED = -1e30
VMEM_LIMIT = 56 * 1024 * 1024

ADA_TN = 1024
PROJ_TM = 1024
PROJ_TN = 512
OUT_TM = 2048
ROUTE_TM = 256
DISP_TM = 256
MOE_ROWS = 128
WEIGHT_SLOTS = 3
N_SLOTS = N_TOK * TOP_K
N_ROW_BLOCKS = N_SLOTS // MOE_ROWS
MAX_ITEMS = N_ROW_BLOCKS + N_EXPERTS
COMB_TM = 256


def _dot(a, b):
    return jnp.dot(a, b, preferred_element_type=F32)


def _dot_nt(a, b):
    return lax.dot_general(a, b, (((1,), (1,)), ((), ())), preferred_element_type=F32)


def _params(sem):
    return pltpu.CompilerParams(dimension_semantics=sem, vmem_limit_bytes=VMEM_LIMIT)


PACKED = D_MODEL // 2


def _pack_row_halves(v):
    lo = pltpu.bitcast(v[:, :PACKED].astype(BF16).astype(F32), U32)
    hi = pltpu.bitcast(v[:, PACKED:].astype(BF16).astype(F32), U32)
    return hi | (lo >> 16)


def _unpack_row_halves(w):
    lo = pltpu.bitcast(w << 16, F32)
    hi = pltpu.bitcast(w & jnp.uint32(0xFFFF0000), F32)
    return lo, hi


def _adaln_kernel(c_ref, w_ref, b_ref, o_ref):
    c = c_ref[...]
    o_ref[...] = _dot(c * jax.nn.sigmoid(c), w_ref[...]) + b_ref[...]


def _adaln(c8, w_ada, b_ada):
    n = w_ada.shape[-1]
    return pl.pallas_call(
        _adaln_kernel,
        grid=(n // ADA_TN,),
        in_specs=[
            pl.BlockSpec((8, D_MODEL), lambda j: (0, 0)),
            pl.BlockSpec((None, D_MODEL, ADA_TN), lambda j: (0, 0, j)),
            pl.BlockSpec((1, ADA_TN), lambda j: (0, j)),
        ],
        out_specs=pl.BlockSpec((8, ADA_TN), lambda j: (0, j)),
        out_shape=jax.ShapeDtypeStruct((8, n), F32),
        compiler_params=_params(("arbitrary",)),
        name="adaln",
    )(c8, w_ada, b_ada)


def _in_proj_kernel(x_ref, mod_ref, g_ref, w_ref, qkg_ref, o_ref, h_ref):
    j = pl.program_id(1)

    @pl.when(j == 0)
    def _():
        xf = x_ref[...]
        ms = jnp.mean(xf * xf, axis=-1, keepdims=True)
        y = xf * lax.rsqrt(ms + EPS) * g_ref[...]
        h_ref[...] = (y * (1.0 + mod_ref[0, 1:2, :]) + mod_ref[0, 0:1, :]).astype(BF16)

    @pl.when(j < 2 * ATTN_WIDTH // PROJ_TN)
    def _():
        for pair in range(PROJ_TN // MXU_COLS):
            cols = slice(pair * MXU_COLS, (pair + 1) * MXU_COLS)
            acc = _dot(h_ref[...], w_ref[:, cols].astype(BF16))
            for hh in range(MXU_COLS // HEAD_DIM):
                a = acc[:, hh * HEAD_DIM:(hh + 1) * HEAD_DIM]
                sl = slice(pair * MXU_COLS + hh * HEAD_DIM, pair * MXU_COLS + (hh + 1) * HEAD_DIM)
                ms = jnp.mean(a * a, axis=-1, keepdims=True)
                o_ref[:, sl] = (a * lax.rsqrt(ms + EPS) * qkg_ref[:, sl]).astype(BF16)

    @pl.when(j >= 2 * ATTN_WIDTH // PROJ_TN)
    def _():
        o_ref[...] = _dot(h_ref[...], w_ref[...].astype(BF16)).astype(BF16)


def _in_proj(x2, mod3, norm1_g, w_in, qk_gain):
    n_qk = 2 * ATTN_WIDTH // PROJ_TN
    rows_per_batch = SEQ // PROJ_TM
    return pl.pallas_call(
        _in_proj_kernel,
        grid=(N_TOK // PROJ_TM, IN_WIDTH // PROJ_TN),
        in_specs=[
            pl.BlockSpec((PROJ_TM, D_MODEL), lambda i, j: (i, 0)),
            pl.BlockSpec((1, 6, D_MODEL), lambda i, j: (i // rows_per_batch, 0, 0)),
            pl.BlockSpec((1, D_MODEL), lambda i, j: (0, 0)),
            pl.BlockSpec((None, D_MODEL, PROJ_TN), lambda i, j: (0, 0, j)),
            pl.BlockSpec((1, PROJ_TN), lambda i, j: (0, jnp.minimum(j, n_qk - 1))),
        ],
        out_specs=pl.BlockSpec((PROJ_TM, PROJ_TN), lambda i, j: (i, j)),
        out_shape=jax.ShapeDtypeStruct((N_TOK, IN_WIDTH), BF16),
        scratch_shapes=[pltpu.VMEM((PROJ_TM, D_MODEL), BF16)],
        compiler_params=_params(("arbitrary", "arbitrary")),
        name="in_proj",
    )(x2, mod3, norm1_g, w_in, qk_gain)


def _mixer_kernel(q_ref, k_ref, v_ref, u_ref, bg_ref, cg_ref, cw_ref, ag_ref, yg_ref,
                  a_ref, y_ref, s_ref):
    blk = MOBA_BLOCK
    k = k_ref[...]
    v = v_ref[...]

    km = jnp.mean(k.astype(F32).reshape(MOBA_NB, blk, HEAD_DIM), axis=1)
    km_hi = km.astype(BF16)
    km_lo = (km - km_hi.astype(F32)).astype(BF16)
    zpad = jnp.zeros((LANES - MOBA_NB, HEAD_DIM), BF16)
    km_hi = jnp.concatenate([km_hi, zpad], axis=0)
    km_lo = jnp.concatenate([km_lo, zpad], axis=0)

    lane = lax.broadcasted_iota(I32, (blk, LANES), 1)
    row = lax.broadcasted_iota(I32, (blk, blk), 0)
    col = lax.broadcasted_iota(I32, (blk, blk), 1)

    for i in range(MOBA_NB):
        qi = q_ref[i * blk:(i + 1) * blk, :]
        if i > 0:
            if i > MOBA_TOPK:
                g = _dot_nt(qi, km_hi) + _dot_nt(qi, km_lo)
                rank = jnp.zeros((blk, LANES), F32)
                for jp in range(i):
                    cj = g[:, jp:jp + 1]
                    tie = jnp.where(lane > jp, 1.0, 0.0)
                    rank = rank + jnp.where(cj > g, 1.0, jnp.where(cj == g, tie, 0.0))
                sel = jnp.where(rank < MOBA_TOPK, 1.0, 0.0)
            else:
                sel = jnp.ones((blk, LANES), F32)
            for j in range(i):
                s = _dot_nt(qi, k[j * blk:(j + 1) * blk])
                s_ref[:, j * blk:(j + 1) * blk] = jnp.where(sel[:, j:j + 1] > 0.5, s, MASKED)
        s = _dot_nt(qi, k[i * blk:(i + 1) * blk])
        s_ref[:, i * blk:(i + 1) * blk] = jnp.where(col <= row, s, MASKED)

        width = (i + 1) * blk
        sc = s_ref[:, :width]
        m = jnp.max(sc, axis=-1, keepdims=True)
        p = jnp.exp(sc - m)
        denom = jnp.sum(p, axis=-1, keepdims=True)
        o = _dot(p.astype(BF16), v[:width]) / denom
        ms = jnp.mean(o * o, axis=-1, keepdims=True)
        a_ref[i * blk:(i + 1) * blk, :] = (o * lax.rsqrt(ms + EPS) * ag_ref[...]).astype(BF16)

    z = cg_ref[...].astype(F32) * u_ref[...].astype(F32)
    t = lax.broadcasted_iota(I32, z.shape, 0)
    z1 = jnp.where(t >= 1, pltpu.roll(z, 1, 0), 0.0)
    z2 = jnp.where(t >= 2, pltpu.roll(z, 2, 0), 0.0)
    y = cw_ref[0:1, :] * z2 + cw_ref[1:2, :] * z1 + cw_ref[2:3, :] * z
    y = bg_ref[...].astype(F32) * y
    ms = jnp.mean(y * y, axis=-1, keepdims=True)
    y_ref[...] = (y * lax.rsqrt(ms + EPS) * yg_ref[...]).astype(BF16)


def _mixer(proj, conv_w, attn_out_g, conv_out_g):
    h = ATTN_HEADS

    def col(off):
        return pl.BlockSpec((SEQ, HEAD_DIM), lambda b, g: (b, off + g))

    out_spec = pl.BlockSpec((SEQ, HEAD_DIM), lambda b, g: (b, g))
    vec_spec = pl.BlockSpec((1, HEAD_DIM), lambda b, g: (0, g))
    return pl.pallas_call(
        _mixer_kernel,
        grid=(BATCH, h),
        in_specs=[col(0), col(h), col(2 * h), col(3 * h), col(4 * h), col(5 * h),
                  pl.BlockSpec((None, CONV_K, HEAD_DIM), lambda b, g: (0, 0, g)),
                  vec_spec, vec_spec],
        out_specs=[out_spec, out_spec],
        out_shape=[jax.ShapeDtypeStruct((N_TOK, ATTN_WIDTH), BF16),
                   jax.ShapeDtypeStruct((N_TOK, CONV_WIDTH), BF16)],
        scratch_shapes=[pltpu.VMEM((MOBA_BLOCK, SEQ), F32)],
        compiler_params=_params(("arbitrary", "arbitrary")),
        name="mixer",
    )(proj, proj, proj, proj, proj, proj, conv_w, attn_out_g, conv_out_g)


def _out_proj_kernel(a_ref, y_ref, wa_ref, wb_ref, x_ref, mod_ref, o_ref):
    mix = _dot(a_ref[...], wa_ref[...].astype(BF16)) + _dot(y_ref[...], wb_ref[...].astype(BF16))
    o_ref[...] = x_ref[...] + mod_ref[0, 2:3, :] * mix


def _out_proj(a, y, w_o, x2, mod3):
    tm = OUT_TM
    rows_per_batch = SEQ // tm
    return pl.pallas_call(
        _out_proj_kernel,
        grid=(N_TOK // tm, D_MODEL // PROJ_TN),
        in_specs=[
            pl.BlockSpec((tm, ATTN_WIDTH), lambda i, j: (i, 0)),
            pl.BlockSpec((tm, CONV_WIDTH), lambda i, j: (i, 0)),
            pl.BlockSpec((None, ATTN_WIDTH, PROJ_TN), lambda i, j: (0, 0, j)),
            pl.BlockSpec((None, CONV_WIDTH, PROJ_TN), lambda i, j: (0, 1, j)),
            pl.BlockSpec((tm, PROJ_TN), lambda i, j: (i, j)),
            pl.BlockSpec((1, 6, PROJ_TN), lambda i, j: (i // rows_per_batch, 0, j)),
        ],
        out_specs=pl.BlockSpec((tm, PROJ_TN), lambda i, j: (i, j)),
        out_shape=jax.ShapeDtypeStruct((N_TOK, D_MODEL), F32),
        compiler_params=_params(("arbitrary", "arbitrary")),
        name="out_proj",
    )(a, y, w_o, w_o, x2, mod3)


def _norm2_modulate(xf, g_ref, mod_ref):
    ms = jnp.mean(xf * xf, axis=-1, keepdims=True)
    return xf * lax.rsqrt(ms + EPS) * g_ref[...] * (1.0 + mod_ref[0, 4:5, :]) + mod_ref[0, 3:4, :]


def _route_kernel(x_ref, g_ref, mod_ref, wrt_ref, bias_ref,
                  idx_ref, wt_ref, pos_ref, cnt_ref, carry_ref):
    tm = ROUTE_TM
    ne = N_EXPERTS

    @pl.when(pl.program_id(0) == 0)
    def _():
        carry_ref[...] = jnp.zeros_like(carry_ref)

    hb = _norm2_modulate(x_ref[...], g_ref, mod_ref).astype(BF16)

    scores = jax.nn.sigmoid(_dot_nt(wrt_ref[...].astype(BF16), hb))
    choice = scores + bias_ref[:, 0:1]
    ninf = -jnp.inf

    gi = lax.broadcasted_iota(I32, (GROUP_SIZE, tm), 0)
    rows = []
    for g in range(N_GROUPS):
        blk = choice[g * GROUP_SIZE:(g + 1) * GROUP_SIZE, :]
        m1 = jnp.max(blk, axis=0, keepdims=True)
        i1 = jnp.min(jnp.where(blk == m1, gi, GROUP_SIZE), axis=0, keepdims=True)
        m2 = jnp.max(jnp.where(gi == i1, ninf, blk), axis=0, keepdims=True)
        rows.append(m1 + m2)
    gsc = jnp.concatenate(rows, axis=0)

    gidx = lax.broadcasted_iota(I32, (N_GROUPS, tm), 0)
    rank = jnp.zeros((N_GROUPS, tm), F32)
    for gp in range(N_GROUPS):
        r = gsc[gp:gp + 1, :]
        tie = jnp.where(gidx > gp, 1.0, 0.0)
        rank = rank + jnp.where(r > gsc, 1.0, jnp.where(r == gsc, tie, 0.0))
    gsel = jnp.where(rank < TOPK_GROUPS, 1.0, 0.0)
    esel = jnp.concatenate(
        [jnp.broadcast_to(gsel[g:g + 1, :], (GROUP_SIZE, tm)) for g in range(N_GROUPS)], axis=0)
    masked = jnp.where(esel > 0.5, choice, ninf)

    eidx = lax.broadcasted_iota(I32, (ne, tm), 0)
    idx_rows, w_rows = [], []
    onehot = jnp.zeros((ne, tm), F32)
    for _ in range(TOP_K):
        m = jnp.max(masked, axis=0, keepdims=True)
        sel = jnp.min(jnp.where(masked == m, eidx, ne), axis=0, keepdims=True)
        hit = eidx == sel
        idx_rows.append(sel)
        w_rows.append(jnp.sum(jnp.where(hit, scores, 0.0), axis=0, keepdims=True))
        masked = jnp.where(hit, ninf, masked)
        onehot = jnp.where(hit, 1.0, onehot)
    wsel = jnp.concatenate(w_rows, axis=0)
    idx_ref[...] = jnp.concatenate(idx_rows, axis=0)
    wt_ref[...] = wsel / jnp.sum(wsel, axis=0, keepdims=True) * ROUTED_SCALE

    ti = lax.broadcasted_iota(I32, (tm, tm), 0)
    tj = lax.broadcasted_iota(I32, (tm, tm), 1)
    upper = jnp.where(ti < tj, 1.0, 0.0).astype(BF16)
    before = _dot(onehot.astype(BF16), upper) + carry_ref[:, 0:1]
    pos_rows = [jnp.sum(jnp.where(eidx == idx_rows[kk], before, 0.0), axis=0, keepdims=True)
                for kk in range(TOP_K)]
    pos_ref[...] = jnp.concatenate(pos_rows, axis=0).astype(I32)
    carry_ref[...] = carry_ref[...] + jnp.sum(onehot, axis=1, keepdims=True)
    cnt_ref[...] = carry_ref[...].astype(I32)


def _route(x1, norm2_g, mod3, w_rt, bias_col):
    tm = ROUTE_TM
    rows_per_batch = SEQ // tm
    full = lambda shape: pl.BlockSpec(shape, lambda i: tuple(0 for _ in shape))
    tok_tile = pl.BlockSpec((TOP_K, tm), lambda i: (0, i))
    return pl.pallas_call(
        _route_kernel,
        grid=(N_TOK // tm,),
        in_specs=[
            pl.BlockSpec((tm, D_MODEL), lambda i: (i, 0)),
            full((1, D_MODEL)),
            pl.BlockSpec((1, 6, D_MODEL), lambda i: (i // rows_per_batch, 0, 0)),
            full((N_EXPERTS, D_MODEL)),
            full((N_EXPERTS, LANES)),
        ],
        out_specs=[tok_tile, tok_tile, tok_tile, full((N_EXPERTS, LANES))],
        out_shape=[
            jax.ShapeDtypeStruct((TOP_K, N_TOK), I32),
            jax.ShapeDtypeStruct((TOP_K, N_TOK), F32),
            jax.ShapeDtypeStruct((TOP_K, N_TOK), I32),
            jax.ShapeDtypeStruct((N_EXPERTS, LANES), I32),
        ],
        scratch_shapes=[pltpu.VMEM((N_EXPERTS, LANES), F32)],
        compiler_params=_params(("arbitrary",)),
        name="route",
    )(x1, norm2_g, mod3, w_rt, bias_col)


def _shared_kernel(x_ref, g_ref, mod_ref, wsg_ref, wsu_ref, wsd_ref, hf_ref, base_ref, wsg_bf, wsu_bf, wsd_bf):
    @pl.when(pl.program_id(0) == 0)
    def _():
        wsg_bf[...] = wsg_ref[...].astype(BF16)
        wsu_bf[...] = wsu_ref[...].astype(BF16)
        wsd_bf[...] = wsd_ref[...].astype(BF16)

    xf = x_ref[...]
    hf = _norm2_modulate(xf, g_ref, mod_ref)
    hf_ref[...] = _pack_row_halves(hf)

    hb = hf.astype(BF16)
    hg = _dot(hb, wsg_bf[...])
    hu = _dot(hb, wsu_bf[...])
    hid = (hg * jax.nn.sigmoid(hg)) * hu
    shared = _dot(hid.astype(BF16), wsd_bf[...])
    base_ref[...] = xf + mod_ref[0, 5:6, :] * shared


def _shared(x1, norm2_g, mod3, ws_gate, ws_up, ws_down):
    tm = DISP_TM
    rows_per_batch = SEQ // tm
    row_tile = pl.BlockSpec((tm, D_MODEL), lambda i: (i, 0))
    return pl.pallas_call(
        _shared_kernel,
        grid=(N_TOK // tm,),
        in_specs=[
            row_tile,
            pl.BlockSpec((1, D_MODEL), lambda i: (0, 0)),
            pl.BlockSpec((1, 6, D_MODEL), lambda i: (i // rows_per_batch, 0, 0)),
            pl.BlockSpec((None, D_MODEL, SHARED_DIM), lambda i: (0, 0, 0)),
            pl.BlockSpec((None, D_MODEL, SHARED_DIM), lambda i: (0, 0, 0)),
            pl.BlockSpec((None, SHARED_DIM, D_MODEL), lambda i: (0, 0, 0)),
        ],
        out_specs=[pl.BlockSpec((tm, PACKED), lambda i: (i, 0)), row_tile],
        out_shape=[jax.ShapeDtypeStruct((N_TOK, PACKED), U32),
                   jax.ShapeDtypeStruct((N_TOK, D_MODEL), F32)],
        scratch_shapes=[
            pltpu.VMEM((D_MODEL, SHARED_DIM), BF16),
            pltpu.VMEM((D_MODEL, SHARED_DIM), BF16),
            pltpu.VMEM((SHARED_DIM, D_MODEL), BF16),
        ],
        compiler_params=_params(("arbitrary",)),
        name="shared",
    )(x1, norm2_g, mod3, ws_gate, ws_up, ws_down)


DIGIT_BITS = 7
ROW_BITS = 7


def _invperm_kernel(start_ref, idx_ref, pos_ref, tok_ref, dst_ref, acc_ref):
    tm = ROUTE_TM
    step = pl.program_id(0)

    @pl.when(step == 0)
    def _():
        acc_ref[...] = jnp.zeros_like(acc_ref)

    eidx = lax.broadcasted_iota(I32, (N_EXPERTS, tm), 0)
    bidx = lax.broadcasted_iota(I32, (N_ROW_BLOCKS, tm), 0)
    ridx = lax.broadcasted_iota(I32, (MOE_ROWS, tm), 0)
    tok = step * tm + lax.broadcasted_iota(I32, (1, tm), 1)
    start = start_ref[:, 0:1]
    n_digits = -(-(TOP_K * N_TOK).bit_length() // DIGIT_BITS)

    acc = acc_ref[...]
    for kk in range(TOP_K):
        first = jnp.sum(jnp.where(eidx == idx_ref[kk:kk + 1, :], start, 0.0), axis=0, keepdims=True)
        slot = first.astype(I32) + pos_ref[kk:kk + 1, :]
        in_blk = bidx == (slot >> ROW_BITS)
        in_row = jnp.where(ridx == (slot & (MOE_ROWS - 1)), 1.0, 0.0).astype(BF16)
        code = kk * N_TOK + tok + 1
        for d in range(n_digits):
            digit = ((code >> (d * DIGIT_BITS)) & ((1 << DIGIT_BITS) - 1)).astype(F32)
            part = _dot_nt(jnp.where(in_blk, digit, 0.0).astype(BF16), in_row)
            acc = acc + float(1 << (d * DIGIT_BITS)) * part
    acc_ref[...] = acc
    dst = acc.astype(I32) - 1
    dst_ref[...] = dst
    tok_ref[...] = dst & (N_TOK - 1)


def _invperm(start_col, idx_t, pos_t):
    tm = ROUTE_TM
    tok_tile = pl.BlockSpec((TOP_K, tm), lambda i: (0, i))
    table = pl.BlockSpec((N_ROW_BLOCKS, MOE_ROWS), lambda i: (0, 0))
    table_shape = jax.ShapeDtypeStruct((N_ROW_BLOCKS, MOE_ROWS), I32)
    return pl.pallas_call(
        _invperm_kernel,
        grid=(N_TOK // tm,),
        in_specs=[pl.BlockSpec((N_EXPERTS, LANES), lambda i: (0, 0)), tok_tile, tok_tile],
        out_specs=[table, table],
        out_shape=[table_shape, table_shape],
        scratch_shapes=[pltpu.VMEM((N_ROW_BLOCKS, MOE_ROWS), F32)],
        compiler_params=_params(("arbitrary",)),
        name="invperm",
    )(start_col, idx_t, pos_t)


def _weight_copies(hbm_refs, expert, bufs, slot, sems):
    return [pltpu.make_async_copy(h.at[0, expert], b.at[slot], sems.at[slot, j])
            for j, (h, b) in enumerate(zip(hbm_refs, bufs))]


def _for_each_row(fn):
    def tile(r8, carry):
        r0 = pl.multiple_of(r8 * SUBLANES, SUBLANES)
        for j in range(SUBLANES):
            fn(r0 + j)
        return carry

    lax.fori_loop(0, MOE_ROWS // SUBLANES, tile, 0)


def _experts_kernel(blk_ref, exp_ref, lo_ref, hi_ref, first_ref, par_ref, nxt_ref, nxt2_ref, n_ref,
                    tok_ref, tok_next_ref, dst_ref, dst_prev_ref, hf_hbm, wg_hbm, wu_hbm, wd_hbm, ys_hbm,
                    x_buf, o_buf, wg_buf, wu_buf, wd_buf, sems, gsem, ssem):
    i = pl.program_id(0)
    hbm = (wg_hbm, wu_hbm, wd_hbm)
    bufs = (wg_buf, wu_buf, wd_buf)
    n_items = n_ref[0]
    active = i < n_items
    expert = exp_ref[i]
    slot = par_ref[i]
    blk = blk_ref[i]
    bslot = blk & 1
    lo = lo_ref[i]
    hi = hi_ref[i]

    def gather_copy(tok, r, s):
        return pltpu.make_async_copy(hf_hbm.at[pl.ds(tok, 1), :], x_buf.at[s, pl.ds(r, 1), :], gsem.at[s])

    def scatter_copy(r, dst, s):
        return pltpu.make_async_copy(o_buf.at[s, pl.ds(r, 1), :], ys_hbm.at[pl.ds(dst, 1), :], ssem.at[s])

    def start_gather(table, s):
        _for_each_row(lambda r: gather_copy(table[0, 0, r], r, s).start())

    def wait_gather(s):
        _for_each_row(lambda r: gather_copy(0, 0, s).wait())

    def wait_scatter(s):
        _for_each_row(lambda r: scatter_copy(0, 0, s).wait())

    @pl.when(i == 0)
    def _():
        for cp in _weight_copies(hbm, expert, bufs, 0, sems):
            cp.start(priority=1)

        @pl.when(nxt_ref[0] >= 0)
        def _():
            for cp in _weight_copies(hbm, nxt_ref[0], bufs, 1, sems):
                cp.start(priority=1)

        start_gather(tok_ref, bslot)

    first_of_block = jnp.logical_and(active, lo == 0)
    interior = jnp.logical_and(blk >= 1, blk + 1 < N_ROW_BLOCKS)

    @pl.when(first_of_block)
    def _():
        wait_gather(bslot)

        @pl.when(blk >= 2)
        def _():
            wait_scatter(bslot)

        @pl.when(blk == 0)
        def _():
            start_gather(tok_next_ref, 1 - bslot)

        @pl.when(blk == N_ROW_BLOCKS - 1)
        def _():
            _for_each_row(lambda r: scatter_copy(r, dst_prev_ref[0, 0, r], 1 - bslot).start())

    @pl.when(jnp.logical_and(active, first_ref[i] == 1))
    def _():
        for cp in _weight_copies(hbm, expert, bufs, slot, sems):
            cp.wait()
        ahead = nxt2_ref[i]

        @pl.when(ahead >= 0)
        def _():
            for cp in _weight_copies(hbm, ahead, bufs, jnp.where(slot == 0, WEIGHT_SLOTS - 1, slot - 1), sems):
                cp.start(priority=1)

    def ffn(first_owner, row_copies=(lambda: None, lambda: None)):
        x = jnp.concatenate(_unpack_row_halves(x_buf[bslot]), axis=1)
        hg = _dot(x, wg_buf[slot])
        hu = _dot(x, wu_buf[slot])
        hid = (hg * jax.nn.sigmoid(hg)) * hu
        y = _pack_row_halves(_dot(hid, wd_buf[slot]))
        r = lax.broadcasted_iota(I32, (MOE_ROWS, 1), 0)
        mine = (r >= lo) & (r < hi)
        prev = None if first_owner else o_buf[bslot]
        row_copies[0]()
        row_copies[1]()
        o_buf[bslot] = jnp.where(mine, y, jnp.uint32(0) if first_owner else prev)

    @pl.when(jnp.logical_and(first_of_block, interior))
    def _():
        def send_previous():
            for r in range(MOE_ROWS):
                scatter_copy(r, dst_prev_ref[0, 0, r], 1 - bslot).start()

        def fetch_next():
            for r in range(MOE_ROWS):
                gather_copy(tok_next_ref[0, 0, r], r, 1 - bslot).start()

        ffn(True, (send_previous, fetch_next))

    @pl.when(jnp.logical_and(first_of_block, jnp.logical_not(interior)))
    def _():
        ffn(True)

    @pl.when(jnp.logical_and(active, lo > 0))
    def _():
        ffn(False)

    @pl.when(i == n_items - 1)
    def _():
        _for_each_row(lambda r: scatter_copy(r, dst_ref[0, 0, r], bslot).start())
        wait_scatter(1 - bslot)
        wait_scatter(bslot)


def _experts(items, tok, dst, hf, w_gate, w_up, w_down):
    table = lambda step: pl.BlockSpec(
        (1, 1, MOE_ROWS), lambda i, blk, *_: (jnp.clip(blk[i] + step, 0, N_ROW_BLOCKS - 1), 0, 0),
        memory_space=pltpu.SMEM)
    hbm = pl.BlockSpec(memory_space=pl.ANY)
    grid_spec = pltpu.PrefetchScalarGridSpec(
        num_scalar_prefetch=len(items),
        grid=(MAX_ITEMS,),
        in_specs=[table(0), table(1), table(0), table(-1), hbm, hbm, hbm, hbm],
        out_specs=hbm,
        scratch_shapes=[
            pltpu.VMEM((2, MOE_ROWS, PACKED), U32),
            pltpu.VMEM((2, MOE_ROWS, PACKED), U32),
            pltpu.VMEM((WEIGHT_SLOTS, D_MODEL, EXPERT_DIM), F32),
            pltpu.VMEM((WEIGHT_SLOTS, D_MODEL, EXPERT_DIM), F32),
            pltpu.VMEM((WEIGHT_SLOTS, EXPERT_DIM, D_MODEL), F32),
            pltpu.SemaphoreType.DMA((WEIGHT_SLOTS, 3)),
            pltpu.SemaphoreType.DMA((2,)),
            pltpu.SemaphoreType.DMA((2,)),
        ],
    )
    tok3 = tok.reshape(N_ROW_BLOCKS, 1, MOE_ROWS)
    dst3 = dst.reshape(N_ROW_BLOCKS, 1, MOE_ROWS)
    return pl.pallas_call(
        _experts_kernel,
        grid_spec=grid_spec,
        out_shape=jax.ShapeDtypeStruct((N_SLOTS, PACKED), U32),
        compiler_params=_params(("arbitrary",)),
        name="experts",
    )(*items, tok3, tok3, dst3, dst3, hf, w_gate, w_up, w_down)


def _combine_kernel(wt_ref, base_ref, mod_ref, ys_ref, o_ref):
    w_tok = wt_ref[...].T
    acc_lo = acc_hi = None
    for kk in range(TOP_K):
        lo, hi = _unpack_row_halves(ys_ref[kk])
        w = w_tok[:, kk:kk + 1]
        acc_lo = w * lo if kk == 0 else acc_lo + w * lo
        acc_hi = w * hi if kk == 0 else acc_hi + w * hi
    o_ref[:, :PACKED] = base_ref[:, :PACKED] + mod_ref[0, 5:6, :PACKED] * acc_lo
    o_ref[:, PACKED:] = base_ref[:, PACKED:] + mod_ref[0, 5:6, PACKED:] * acc_hi


def _combine(w_t, base, mod3, ys):
    tm = COMB_TM
    rows_per_batch = SEQ // tm
    return pl.pallas_call(
        _combine_kernel,
        grid=(N_TOK // tm,),
        in_specs=[
            pl.BlockSpec((TOP_K, tm), lambda i: (0, i)),
            pl.BlockSpec((tm, D_MODEL), lambda i: (i, 0)),
            pl.BlockSpec((1, 6, D_MODEL), lambda i: (i // rows_per_batch, 0, 0)),
            pl.BlockSpec((TOP_K, tm, PACKED), lambda i: (0, i, 0)),
        ],
        out_specs=pl.BlockSpec((tm, D_MODEL), lambda i: (i, 0)),
        out_shape=jax.ShapeDtypeStruct((N_TOK, D_MODEL), F32),
        compiler_params=_params(("arbitrary",)),
        name="combine",
    )(w_t, base, mod3, ys.reshape(TOP_K, N_TOK, PACKED))


def _work_items(counts):
    rows = MOE_ROWS
    grp_end = jnp.cumsum(counts)
    grp_start = grp_end - counts
    first_blk = grp_start // rows
    last_blk = (grp_end - 1) // rows
    n_blk = jnp.where(counts > 0, last_blk - first_blk + 1, 0)
    item_end = jnp.cumsum(n_blk)
    item_start = item_end - n_blk
    n_items = item_end[-1]

    ids = jnp.arange(N_EXPERTS, dtype=I32)
    nonempty = (n_blk > 0).astype(I32)
    ordinal = jnp.cumsum(nonempty) - nonempty
    later = lax.cummin(jnp.where(nonempty > 0, ids, N_EXPERTS), axis=0, reverse=True)
    nxt_e = jnp.concatenate([later[1:], jnp.full((1,), N_EXPERTS, I32)])
    nxt_e = jnp.where(nxt_e >= N_EXPERTS, -1, nxt_e)
    nxt2_e = jnp.where(nxt_e >= 0, jnp.take(nxt_e, jnp.maximum(nxt_e, 0)), -1)

    i = jnp.minimum(jnp.arange(MAX_ITEMS, dtype=I32), n_items - 1)
    own = (item_start[None, :] <= i[:, None]) & (i[:, None] < item_end[None, :])
    pick = lambda v: jnp.sum(jnp.where(own, v[None, :], 0), axis=1).astype(I32)
    blk = pick(first_blk - item_start) + i
    lo = jnp.maximum(pick(grp_start), blk * rows) - blk * rows
    hi = jnp.minimum(pick(grp_end), (blk + 1) * rows) - blk * rows
    first = (pick(item_start) == i).astype(I32)
    items = (blk, pick(ids), lo, hi, first, pick(ordinal) % WEIGHT_SLOTS, pick(nxt_e), pick(nxt2_e),
             n_items.astype(I32).reshape(1))
    return grp_start.astype(I32), items


def kernel(x, c, w_ada, b_ada, norm1_g, w_in, q_norm_g, k_norm_g, conv_w, attn_out_g, conv_out_g,
           w_o, norm2_g, w_router, router_bias, w_gate, w_up, w_down, ws_gate, ws_up, ws_down):
    b, s, d = x.shape
    x2 = x.reshape(b * s, d)

    c8 = jnp.pad(c, ((0, 8 - b), (0, 0)))
    mod3 = _adaln(c8, w_ada, b_ada)[:b].reshape(b, 6, d)

    qk_gain = jnp.concatenate([jnp.tile(q_norm_g[0] * HEAD_DIM ** -0.5, ATTN_HEADS),
                               jnp.tile(k_norm_g[0], ATTN_HEADS)]).reshape(1, 2 * ATTN_WIDTH)
    proj = _in_proj(x2, mod3, norm1_g, w_in, qk_gain)

    a, y = _mixer(proj, conv_w, attn_out_g, conv_out_g)
    x1 = _out_proj(a, y, w_o, x2, mod3)

    w_rt = w_router[0].T
    bias_col = jnp.broadcast_to(router_bias[0][:, None], (N_EXPERTS, LANES))
    idx_t, w_t, pos_t, cnt = _route(x1, norm2_g, mod3, w_rt, bias_col)

    grp_start, items = _work_items(cnt[:, 0])
    start_col = jnp.broadcast_to(grp_start.astype(F32)[:, None], (N_EXPERTS, LANES))
    slot_tok, slot_dst = _invperm(start_col, idx_t, pos_t)
    hf, base = _shared(x1, norm2_g, mod3, ws_gate, ws_up, ws_down)
    ys = _experts(items, slot_tok, slot_dst, hf, w_gate, w_up, w_down)
    out = _combine(w_t, base, mod3, ys)
    return out.reshape(b, s, d)
```

```python
import functools

import jax
import jax.numpy as jnp
from jax import lax
from jax.experimental import pallas as pl
from jax.experimental.pallas import tpu as pltpu

F32 = jnp.float32
BF16 = jnp.bfloat16
I32 = jnp.int32
U32 = jnp.uint32

D_MODEL = 2048
BATCH = 4
SEQ = 2048
N_TOK = BATCH * SEQ
HEAD_DIM = 128
ATTN_WIDTH = 1024
ATTN_HEADS = 8
CONV_WIDTH = 1024
CONV_GROUPS = 8
IN_WIDTH = 6144
CONV_K = 3
MOBA_BLOCK = 256
MOBA_NB = SEQ // MOBA_BLOCK
MOBA_TOPK = 3
N_EXPERTS = 256
TOP_K = 8
N_GROUPS = 8
GROUP_SIZE = N_EXPERTS // N_GROUPS
TOPK_GROUPS = 4
EXPERT_DIM = 512
SHARED_DIM = 512
ROUTED_SCALE = 2.5
EPS = 1e-6

LANES = 128
SUBLANES = 8
MXU_COLS = 256
MASKED = -1e30
VMEM_LIMIT = 56 * 1024 * 1024

ADA_TN = 1024
PROJ_TM = 1024
PROJ_TN = 512
OUT_TM = 2048
ROUTE_TM = 256
DISP_TM = 256
MOE_ROWS = 128
WEIGHT_SLOTS = 3
N_SLOTS = N_TOK * TOP_K
N_ROW_BLOCKS = N_SLOTS // MOE_ROWS
MAX_ITEMS = N_ROW_BLOCKS + N_EXPERTS
COMB_TM = 256


def _dot(a, b):
    return jnp.dot(a, b, preferred_element_type=F32)


def _dot_nt(a, b):
    return lax.dot_general(a, b, (((1,), (1,)), ((), ())), preferred_element_type=F32)


def _params(sem):
    return pltpu.CompilerParams(dimension_semantics=sem, vmem_limit_bytes=VMEM_LIMIT)


PACKED = D_MODEL // 2


def _pack_row_halves(v):
    lo = pltpu.bitcast(v[:, :PACKED].astype(BF16).astype(F32), U32)
    hi = pltpu.bitcast(v[:, PACKED:].astype(BF16).astype(F32), U32)
    return hi | (lo >> 16)


def _unpack_row_halves(w):
    lo = pltpu.bitcast(w << 16, F32)
    hi = pltpu.bitcast(w & jnp.uint32(0xFFFF0000), F32)
    return lo, hi


def _adaln_kernel(c_ref, w_ref, b_ref, o_ref):
    c = c_ref[...]
    o_ref[...] = _dot(c * jax.nn.sigmoid(c), w_ref[...]) + b_ref[...]


def _adaln(c8, w_ada, b_ada):
    n = w_ada.shape[-1]
    return pl.pallas_call(
        _adaln_kernel,
        grid=(n // ADA_TN,),
        in_specs=[
            pl.BlockSpec((8, D_MODEL), lambda j: (0, 0)),
            pl.BlockSpec((None, D_MODEL, ADA_TN), lambda j: (0, 0, j)),
            pl.BlockSpec((1, ADA_TN), lambda j: (0, j)),
        ],
        out_specs=pl.BlockSpec((8, ADA_TN), lambda j: (0, j)),
        out_shape=jax.ShapeDtypeStruct((8, n), F32),
        compiler_params=_params(("arbitrary",)),
        name="adaln",
    )(c8, w_ada, b_ada)


def _in_proj_kernel(x_ref, mod_ref, g_ref, w_ref, qkg_ref, o_ref, h_ref):
    j = pl.program_id(1)

    @pl.when(j == 0)
    def _():
        xf = x_ref[...]
        ms = jnp.mean(xf * xf, axis=-1, keepdims=True)
        y = xf * lax.rsqrt(ms + EPS) * g_ref[...]
        h_ref[...] = (y * (1.0 + mod_ref[0, 1:2, :]) + mod_ref[0, 0:1, :]).astype(BF16)

    @pl.when(j < 2 * ATTN_WIDTH // PROJ_TN)
    def _():
        for pair in range(PROJ_TN // MXU_COLS):
            cols = slice(pair * MXU_COLS, (pair + 1) * MXU_COLS)
            acc = _dot(h_ref[...], w_ref[:, cols].astype(BF16))
            for hh in range(MXU_COLS // HEAD_DIM):
                a = acc[:, hh * HEAD_DIM:(hh + 1) * HEAD_DIM]
                sl = slice(pair * MXU_COLS + hh * HEAD_DIM, pair * MXU_COLS + (hh + 1) * HEAD_DIM)
                ms = jnp.mean(a * a, axis=-1, keepdims=True)
                o_ref[:, sl] = (a * lax.rsqrt(ms + EPS) * qkg_ref[:, sl]).astype(BF16)

    @pl.when(j >= 2 * ATTN_WIDTH // PROJ_TN)
    def _():
        o_ref[...] = _dot(h_ref[...], w_ref[...].astype(BF16)).astype(BF16)


def _in_proj(x2, mod3, norm1_g, w_in, qk_gain):
    n_qk = 2 * ATTN_WIDTH // PROJ_TN
    rows_per_batch = SEQ // PROJ_TM
    return pl.pallas_call(
        _in_proj_kernel,
        grid=(N_TOK // PROJ_TM, IN_WIDTH // PROJ_TN),
        in_specs=[
            pl.BlockSpec((PROJ_TM, D_MODEL), lambda i, j: (i, 0)),
            pl.BlockSpec((1, 6, D_MODEL), lambda i, j: (i // rows_per_batch, 0, 0)),
            pl.BlockSpec((1, D_MODEL), lambda i, j: (0, 0)),
            pl.BlockSpec((None, D_MODEL, PROJ_TN), lambda i, j: (0, 0, j)),
            pl.BlockSpec((1, PROJ_TN), lambda i, j: (0, jnp.minimum(j, n_qk - 1))),
        ],
        out_specs=pl.BlockSpec((PROJ_TM, PROJ_TN), lambda i, j: (i, j)),
        out_shape=jax.ShapeDtypeStruct((N_TOK, IN_WIDTH), BF16),
        scratch_shapes=[pltpu.VMEM((PROJ_TM, D_MODEL), BF16)],
        compiler_params=_params(("arbitrary", "arbitrary")),
        name="in_proj",
    )(x2, mod3, norm1_g, w_in, qk_gain)


def _mixer_kernel(q_ref, k_ref, v_ref, u_ref, bg_ref, cg_ref, cw_ref, ag_ref, yg_ref,
                  a_ref, y_ref, s_ref):
    blk = MOBA_BLOCK
    k = k_ref[...]
    v = v_ref[...]

    km = jnp.mean(k.astype(F32).reshape(MOBA_NB, blk, HEAD_DIM), axis=1)
    km_hi = km.astype(BF16)
    km_lo = (km - km_hi.astype(F32)).astype(BF16)
    zpad = jnp.zeros((LANES - MOBA_NB, HEAD_DIM), BF16)
    km_hi = jnp.concatenate([km_hi, zpad], axis=0)
    km_lo = jnp.concatenate([km_lo, zpad], axis=0)

    lane = lax.broadcasted_iota(I32, (blk, LANES), 1)
    row = lax.broadcasted_iota(I32, (blk, blk), 0)
    col = lax.broadcasted_iota(I32, (blk, blk), 1)

    for i in range(MOBA_NB):
        qi = q_ref[i * blk:(i + 1) * blk, :]
        if i > 0:
            if i > MOBA_TOPK:
                g = _dot_nt(qi, km_hi) + _dot_nt(qi, km_lo)
                rank = jnp.zeros((blk, LANES), F32)
                for jp in range(i):
                    cj = g[:, jp:jp + 1]
                    tie = jnp.where(lane > jp, 1.0, 0.0)
                    rank = rank + jnp.where(cj > g, 1.0, jnp.where(cj == g, tie, 0.0))
                sel = jnp.where(rank < MOBA_TOPK, 1.0, 0.0)
            else:
                sel = jnp.ones((blk, LANES), F32)
            for j in range(i):
                s = _dot_nt(qi, k[j * blk:(j + 1) * blk])
                s_ref[:, j * blk:(j + 1) * blk] = jnp.where(sel[:, j:j + 1] > 0.5, s, MASKED)
        s = _dot_nt(qi, k[i * blk:(i + 1) * blk])
        s_ref[:, i * blk:(i + 1) * blk] = jnp.where(col <= row, s, MASKED)

        width = (i + 1) * blk
        sc = s_ref[:, :width]
        m = jnp.max(sc, axis=-1, keepdims=True)
        p = jnp.exp(sc - m)
        denom = jnp.sum(p, axis=-1, keepdims=True)
        o = _dot(p.astype(BF16), v[:width]) / denom
        ms = jnp.mean(o * o, axis=-1, keepdims=True)
        a_ref[i * blk:(i + 1) * blk, :] = (o * lax.rsqrt(ms + EPS) * ag_ref[...]).astype(BF16)

    z = cg_ref[...].astype(F32) * u_ref[...].astype(F32)
    t = lax.broadcasted_iota(I32, z.shape, 0)
    z1 = jnp.where(t >= 1, pltpu.roll(z, 1, 0), 0.0)
    z2 = jnp.where(t >= 2, pltpu.roll(z, 2, 0), 0.0)
    y = cw_ref[0:1, :] * z2 + cw_ref[1:2, :] * z1 + cw_ref[2:3, :] * z
    y = bg_ref[...].astype(F32) * y
    ms = jnp.mean(y * y, axis=-1, keepdims=True)
    y_ref[...] = (y * lax.rsqrt(ms + EPS) * yg_ref[...]).astype(BF16)


def _mixer(proj, conv_w, attn_out_g, conv_out_g):
    h = ATTN_HEADS

    def col(off):
        return pl.BlockSpec((SEQ, HEAD_DIM), lambda b, g: (b, off + g))

    out_spec = pl.BlockSpec((SEQ, HEAD_DIM), lambda b, g: (b, g))
    vec_spec = pl.BlockSpec((1, HEAD_DIM), lambda b, g: (0, g))
    return pl.pallas_call(
        _mixer_kernel,
        grid=(BATCH, h),
        in_specs=[col(0), col(h), col(2 * h), col(3 * h), col(4 * h), col(5 * h),
                  pl.BlockSpec((None, CONV_K, HEAD_DIM), lambda b, g: (0, 0, g)),
                  vec_spec, vec_spec],
        out_specs=[out_spec, out_spec],
        out_shape=[jax.ShapeDtypeStruct((N_TOK, ATTN_WIDTH), BF16),
                   jax.ShapeDtypeStruct((N_TOK, CONV_WIDTH), BF16)],
        scratch_shapes=[pltpu.VMEM((MOBA_BLOCK, SEQ), F32)],
        compiler_params=_params(("arbitrary", "arbitrary")),
        name="mixer",
    )(proj, proj, proj, proj, proj, proj, conv_w, attn_out_g, conv_out_g)


def _out_proj_kernel(a_ref, y_ref, wa_ref, wb_ref, x_ref, mod_ref, o_ref):
    mix = _dot(a_ref[...], wa_ref[...].astype(BF16)) + _dot(y_ref[...], wb_ref[...].astype(BF16))
    o_ref[...] = x_ref[...] + mod_ref[0, 2:3, :] * mix


def _out_proj(a, y, w_o, x2, mod3):
    tm = OUT_TM
    rows_per_batch = SEQ // tm
    return pl.pallas_call(
        _out_proj_kernel,
        grid=(N_TOK // tm, D_MODEL // PROJ_TN),
        in_specs=[
            pl.BlockSpec((tm, ATTN_WIDTH), lambda i, j: (i, 0)),
            pl.BlockSpec((tm, CONV_WIDTH), lambda i, j: (i, 0)),
            pl.BlockSpec((None, ATTN_WIDTH, PROJ_TN), lambda i, j: (0, 0, j)),
            pl.BlockSpec((None, CONV_WIDTH, PROJ_TN), lambda i, j: (0, 1, j)),
            pl.BlockSpec((tm, PROJ_TN), lambda i, j: (i, j)),
            pl.BlockSpec((1, 6, PROJ_TN), lambda i, j: (i // rows_per_batch, 0, j)),
        ],
        out_specs=pl.BlockSpec((tm, PROJ_TN), lambda i, j: (i, j)),
        out_shape=jax.ShapeDtypeStruct((N_TOK, D_MODEL), F32),
        compiler_params=_params(("arbitrary", "arbitrary")),
        name="out_proj",
    )(a, y, w_o, w_o, x2, mod3)


def _norm2_modulate(xf, g_ref, mod_ref):
    ms = jnp.mean(xf * xf, axis=-1, keepdims=True)
    return xf * lax.rsqrt(ms + EPS) * g_ref[...] * (1.0 + mod_ref[0, 4:5, :]) + mod_ref[0, 3:4, :]


def _route_kernel(x_ref, g_ref, mod_ref, wrt_ref, bias_ref,
                  idx_ref, wt_ref, pos_ref, cnt_ref, carry_ref):
    tm = ROUTE_TM
    ne = N_EXPERTS

    @pl.when(pl.program_id(0) == 0)
    def _():
        carry_ref[...] = jnp.zeros_like(carry_ref)

    hb = _norm2_modulate(x_ref[...], g_ref, mod_ref).astype(BF16)

    scores = jax.nn.sigmoid(_dot_nt(wrt_ref[...].astype(BF16), hb))
    choice = scores + bias_ref[:, 0:1]
    ninf = -jnp.inf

    gi = lax.broadcasted_iota(I32, (GROUP_SIZE, tm), 0)
    rows = []
    for g in range(N_GROUPS):
        blk = choice[g * GROUP_SIZE:(g + 1) * GROUP_SIZE, :]
        m1 = jnp.max(blk, axis=0, keepdims=True)
        i1 = jnp.min(jnp.where(blk == m1, gi, GROUP_SIZE), axis=0, keepdims=True)
        m2 = jnp.max(jnp.where(gi == i1, ninf, blk), axis=0, keepdims=True)
        rows.append(m1 + m2)
    gsc = jnp.concatenate(rows, axis=0)

    gidx = lax.broadcasted_iota(I32, (N_GROUPS, tm), 0)
    rank = jnp.zeros((N_GROUPS, tm), F32)
    for gp in range(N_GROUPS):
        r = gsc[gp:gp + 1, :]
        tie = jnp.where(gidx > gp, 1.0, 0.0)
        rank = rank + jnp.where(r > gsc, 1.0, jnp.where(r == gsc, tie, 0.0))
    gsel = jnp.where(rank < TOPK_GROUPS, 1.0, 0.0)
    esel = jnp.concatenate(
        [jnp.broadcast_to(gsel[g:g + 1, :], (GROUP_SIZE, tm)) for g in range(N_GROUPS)], axis=0)
    masked = jnp.where(esel > 0.5, choice, ninf)

    eidx = lax.broadcasted_iota(I32, (ne, tm), 0)
    idx_rows, w_rows = [], []
    onehot = jnp.zeros((ne, tm), F32)
    for _ in range(TOP_K):
        m = jnp.max(masked, axis=0, keepdims=True)
        sel = jnp.min(jnp.where(masked == m, eidx, ne), axis=0, keepdims=True)
        hit = eidx == sel
        idx_rows.append(sel)
        w_rows.append(jnp.sum(jnp.where(hit, scores, 0.0), axis=0, keepdims=True))
        masked = jnp.where(hit, ninf, masked)
        onehot = jnp.where(hit, 1.0, onehot)
    wsel = jnp.concatenate(w_rows, axis=0)
    idx_ref[...] = jnp.concatenate(idx_rows, axis=0)
    wt_ref[...] = wsel / jnp.sum(wsel, axis=0, keepdims=True) * ROUTED_SCALE

    ti = lax.broadcasted_iota(I32, (tm, tm), 0)
    tj = lax.broadcasted_iota(I32, (tm, tm), 1)
    upper = jnp.where(ti < tj, 1.0, 0.0).astype(BF16)
    before = _dot(onehot.astype(BF16), upper) + carry_ref[:, 0:1]
    pos_rows = [jnp.sum(jnp.where(eidx == idx_rows[kk], before, 0.0), axis=0, keepdims=True)
                for kk in range(TOP_K)]
    pos_ref[...] = jnp.concatenate(pos_rows, axis=0).astype(I32)
    carry_ref[...] = carry_ref[...] + jnp.sum(onehot, axis=1, keepdims=True)
    cnt_ref[...] = carry_ref[...].astype(I32)


def _route(x1, norm2_g, mod3, w_rt, bias_col):
    tm = ROUTE_TM
    rows_per_batch = SEQ // tm
    full = lambda shape: pl.BlockSpec(shape, lambda i: tuple(0 for _ in shape))
    tok_tile = pl.BlockSpec((TOP_K, tm), lambda i: (0, i))
    return pl.pallas_call(
        _route_kernel,
        grid=(N_TOK // tm,),
        in_specs=[
            pl.BlockSpec((tm, D_MODEL), lambda i: (i, 0)),
            full((1, D_MODEL)),
            pl.BlockSpec((1, 6, D_MODEL), lambda i: (i // rows_per_batch, 0, 0)),
            full((N_EXPERTS, D_MODEL)),
            full((N_EXPERTS, LANES)),
        ],
        out_specs=[tok_tile, tok_tile, tok_tile, full((N_EXPERTS, LANES))],
        out_shape=[
            jax.ShapeDtypeStruct((TOP_K, N_TOK), I32),
            jax.ShapeDtypeStruct((TOP_K, N_TOK), F32),
            jax.ShapeDtypeStruct((TOP_K, N_TOK), I32),
            jax.ShapeDtypeStruct((N_EXPERTS, LANES), I32),
        ],
        scratch_shapes=[pltpu.VMEM((N_EXPERTS, LANES), F32)],
        compiler_params=_params(("arbitrary",)),
        name="route",
    )(x1, norm2_g, mod3, w_rt, bias_col)


def _shared_kernel(x_ref, g_ref, mod_ref, wsg_ref, wsu_ref, wsd_ref, hf_ref, base_ref, wsg_bf, wsu_bf, wsd_bf):
    @pl.when(pl.program_id(0) == 0)
    def _():
        wsg_bf[...] = wsg_ref[...].astype(BF16)
        wsu_bf[...] = wsu_ref[...].astype(BF16)
        wsd_bf[...] = wsd_ref[...].astype(BF16)

    xf = x_ref[...]
    hf = _norm2_modulate(xf, g_ref, mod_ref)
    hf_ref[...] = _pack_row_halves(hf)

    hb = hf.astype(BF16)
    hg = _dot(hb, wsg_bf[...])
    hu = _dot(hb, wsu_bf[...])
    hid = (hg * jax.nn.sigmoid(hg)) * hu
    shared = _dot(hid.astype(BF16), wsd_bf[...])
    base_ref[...] = xf + mod_ref[0, 5:6, :] * shared


def _shared(x1, norm2_g, mod3, ws_gate, ws_up, ws_down):
    tm = DISP_TM
    rows_per_batch = SEQ // tm
    row_tile = pl.BlockSpec((tm, D_MODEL), lambda i: (i, 0))
    return pl.pallas_call(
        _shared_kernel,
        grid=(N_TOK // tm,),
        in_specs=[
            row_tile,
            pl.BlockSpec((1, D_MODEL), lambda i: (0, 0)),
            pl.BlockSpec((1, 6, D_MODEL), lambda i: (i // rows_per_batch, 0, 0)),
            pl.BlockSpec((None, D_MODEL, SHARED_DIM), lambda i: (0, 0, 0)),
            pl.BlockSpec((None, D_MODEL, SHARED_DIM), lambda i: (0, 0, 0)),
            pl.BlockSpec((None, SHARED_DIM, D_MODEL), lambda i: (0, 0, 0)),
        ],
        out_specs=[pl.BlockSpec((tm, PACKED), lambda i: (i, 0)), row_tile],
        out_shape=[jax.ShapeDtypeStruct((N_TOK, PACKED), U32),
                   jax.ShapeDtypeStruct((N_TOK, D_MODEL), F32)],
        scratch_shapes=[
            pltpu.VMEM((D_MODEL, SHARED_DIM), BF16),
            pltpu.VMEM((D_MODEL, SHARED_DIM), BF16),
            pltpu.VMEM((SHARED_DIM, D_MODEL), BF16),
        ],
        compiler_params=_params(("arbitrary",)),
        name="shared",
    )(x1, norm2_g, mod3, ws_gate, ws_up, ws_down)


DIGIT_BITS = 7
ROW_BITS = 7


def _invperm_kernel(start_ref, idx_ref, pos_ref, tok_ref, dst_ref, acc_ref):
    tm = ROUTE_TM
    step = pl.program_id(0)

    @pl.when(step == 0)
    def _():
        acc_ref[...] = jnp.zeros_like(acc_ref)

    eidx = lax.broadcasted_iota(I32, (N_EXPERTS, tm), 0)
    bidx = lax.broadcasted_iota(I32, (N_ROW_BLOCKS, tm), 0)
    ridx = lax.broadcasted_iota(I32, (MOE_ROWS, tm), 0)
    tok = step * tm + lax.broadcasted_iota(I32, (1, tm), 1)
    start = start_ref[:, 0:1]
    n_digits = -(-(TOP_K * N_TOK).bit_length() // DIGIT_BITS)

    acc = acc_ref[...]
    for kk in range(TOP_K):
        first = jnp.sum(jnp.where(eidx == idx_ref[kk:kk + 1, :], start, 0.0), axis=0, keepdims=True)
        slot = first.astype(I32) + pos_ref[kk:kk + 1, :]
        in_blk = bidx == (slot >> ROW_BITS)
        in_row = jnp.where(ridx == (slot & (MOE_ROWS - 1)), 1.0, 0.0).astype(BF16)
        code = kk * N_TOK + tok + 1
        for d in range(n_digits):
            digit = ((code >> (d * DIGIT_BITS)) & ((1 << DIGIT_BITS) - 1)).astype(F32)
            part = _dot_nt(jnp.where(in_blk, digit, 0.0).astype(BF16), in_row)
            acc = acc + float(1 << (d * DIGIT_BITS)) * part
    acc_ref[...] = acc
    dst = acc.astype(I32) - 1
    dst_ref[...] = dst
    tok_ref[...] = dst & (N_TOK - 1)


def _invperm(start_col, idx_t, pos_t):
    tm = ROUTE_TM
    tok_tile = pl.BlockSpec((TOP_K, tm), lambda i: (0, i))
    table = pl.BlockSpec((N_ROW_BLOCKS, MOE_ROWS), lambda i: (0, 0))
    table_shape = jax.ShapeDtypeStruct((N_ROW_BLOCKS, MOE_ROWS), I32)
    return pl.pallas_call(
        _invperm_kernel,
        grid=(N_TOK // tm,),
        in_specs=[pl.BlockSpec((N_EXPERTS, LANES), lambda i: (0, 0)), tok_tile, tok_tile],
        out_specs=[table, table],
        out_shape=[table_shape, table_shape],
        scratch_shapes=[pltpu.VMEM((N_ROW_BLOCKS, MOE_ROWS), F32)],
        compiler_params=_params(("arbitrary",)),
        name="invperm",
    )(start_col, idx_t, pos_t)


def _weight_copies(hbm_refs, expert, bufs, slot, sems):
    return [pltpu.make_async_copy(h.at[0, expert], b.at[slot], sems.at[slot, j])
            for j, (h, b) in enumerate(zip(hbm_refs, bufs))]


def _for_each_row(fn):
    def tile(r8, carry):
        r0 = pl.multiple_of(r8 * SUBLANES, SUBLANES)
        for j in range(SUBLANES):
            fn(r0 + j)
        return carry

    lax.fori_loop(0, MOE_ROWS // SUBLANES, tile, 0)


def _experts_kernel(blk_ref, exp_ref, lo_ref, hi_ref, first_ref, par_ref, nxt_ref, nxt2_ref, n_ref,
                    tok_ref, tok_next_ref, dst_ref, dst_prev_ref, hf_hbm, wg_hbm, wu_hbm, wd_hbm, ys_hbm,
                    x_buf, o_buf, wg_buf, wu_buf, wd_buf, wg_bf, wu_bf, wd_bf, sems, gsem, ssem):
    i = pl.program_id(0)
    hbm = (wg_hbm, wu_hbm, wd_hbm)
    bufs = (wg_buf, wu_buf, wd_buf)
    n_items = n_ref[0]
    active = i < n_items
    expert = exp_ref[i]
    slot = par_ref[i]
    blk = blk_ref[i]
    bslot = blk & 1
    lo = lo_ref[i]
    hi = hi_ref[i]

    def gather_copy(tok, r, s):
        return pltpu.make_async_copy(hf_hbm.at[pl.ds(tok, 1), :], x_buf.at[s, pl.ds(r, 1), :], gsem.at[s])

    def scatter_copy(r, dst, s):
        return pltpu.make_async_copy(o_buf.at[s, pl.ds(r, 1), :], ys_hbm.at[pl.ds(dst, 1), :], ssem.at[s])

    def start_gather(table, s):
        _for_each_row(lambda r: gather_copy(table[0, 0, r], r, s).start())

    def wait_gather(s):
        _for_each_row(lambda r: gather_copy(0, 0, s).wait())

    def wait_scatter(s):
        _for_each_row(lambda r: scatter_copy(0, 0, s).wait())

    @pl.when(i == 0)
    def _():
        for cp in _weight_copies(hbm, expert, bufs, 0, sems):
            cp.start(priority=1)

        @pl.when(nxt_ref[0] >= 0)
        def _():
            for cp in _weight_copies(hbm, nxt_ref[0], bufs, 1, sems):
                cp.start(priority=1)

        start_gather(tok_ref, bslot)

    first_of_block = jnp.logical_and(active, lo == 0)
    interior = jnp.logical_and(blk >= 1, blk + 1 < N_ROW_BLOCKS)

    @pl.when(first_of_block)
    def _():
        wait_gather(bslot)

        @pl.when(blk >= 2)
        def _():
            wait_scatter(bslot)

        @pl.when(blk == 0)
        def _():
            start_gather(tok_next_ref, 1 - bslot)

        @pl.when(blk == N_ROW_BLOCKS - 1)
        def _():
            _for_each_row(lambda r: scatter_copy(r, dst_prev_ref[0, 0, r], 1 - bslot).start())

    @pl.when(jnp.logical_and(active, first_ref[i] == 1))
    def _():
        for cp in _weight_copies(hbm, expert, bufs, slot, sems):
            cp.wait()
        ahead = nxt2_ref[i]

        @pl.when(ahead >= 0)
        def _():
            for cp in _weight_copies(hbm, ahead, bufs, jnp.where(slot == 0, WEIGHT_SLOTS - 1, slot - 1), sems):
                cp.start(priority=1)

        wg_bf[...] = wg_buf[slot].astype(BF16)
        wu_bf[...] = wu_buf[slot].astype(BF16)
        wd_bf[...] = wd_buf[slot].astype(BF16)

    def ffn(first_owner, s):
        x = jnp.concatenate(_unpack_row_halves(x_buf[s]), axis=1).astype(BF16)
        hg = _dot(x, wg_bf[...])
        hu = _dot(x, wu_bf[...])
        hid = (hg * jax.nn.sigmoid(hg)) * hu
        y = _pack_row_halves(_dot(hid.astype(BF16), wd_bf[...]))
        r = lax.broadcasted_iota(I32, (MOE_ROWS, 1), 0)
        mine = (r >= lo) & (r < hi)
        o_buf[s] = jnp.where(mine, y, jnp.uint32(0) if first_owner else o_buf[s])

    for s in range(2):
        in_slot = bslot == s

        @pl.when(jnp.logical_and(in_slot, jnp.logical_and(first_of_block, interior)))
        def _():
            for r in range(MOE_ROWS):
                scatter_copy(r, dst_prev_ref[0, 0, r], 1 - s).start()
                gather_copy(tok_next_ref[0, 0, r], r, 1 - s).start()
            ffn(True, s)

        @pl.when(jnp.logical_and(in_slot, jnp.logical_and(first_of_block, jnp.logical_not(interior))))
        def _():
            ffn(True, s)

        @pl.when(jnp.logical_and(in_slot, jnp.logical_and(active, lo > 0)))
        def _():
            ffn(False, s)

    @pl.when(i == n_items - 1)
    def _():
        _for_each_row(lambda r: scatter_copy(r, dst_ref[0, 0, r], bslot).start())
        wait_scatter(1 - bslot)
        wait_scatter(bslot)


def _experts(items, tok, dst, hf, w_gate, w_up, w_down):
    table = lambda step: pl.BlockSpec(
        (1, 1, MOE_ROWS), lambda i, blk, *_: (jnp.clip(blk[i] + step, 0, N_ROW_BLOCKS - 1), 0, 0),
        memory_space=pltpu.SMEM)
    hbm = pl.BlockSpec(memory_space=pl.ANY)
    grid_spec = pltpu.PrefetchScalarGridSpec(
        num_scalar_prefetch=len(items),
        grid=(MAX_ITEMS,),
        in_specs=[table(0), table(1), table(0), table(-1), hbm, hbm, hbm, hbm],
        out_specs=hbm,
        scratch_shapes=[
            pltpu.VMEM((2, MOE_ROWS, PACKED), U32),
            pltpu.VMEM((2, MOE_ROWS, PACKED), U32),
            pltpu.VMEM((WEIGHT_SLOTS, D_MODEL, EXPERT_DIM), F32),
            pltpu.VMEM((WEIGHT_SLOTS, D_MODEL, EXPERT_DIM), F32),
            pltpu.VMEM((WEIGHT_SLOTS, EXPERT_DIM, D_MODEL), F32),
            pltpu.VMEM((D_MODEL, EXPERT_DIM), BF16),
            pltpu.VMEM((D_MODEL, EXPERT_DIM), BF16),
            pltpu.VMEM((EXPERT_DIM, D_MODEL), BF16),
            pltpu.SemaphoreType.DMA((WEIGHT_SLOTS, 3)),
            pltpu.SemaphoreType.DMA((2,)),
            pltpu.SemaphoreType.DMA((2,)),
        ],
    )
    tok3 = tok.reshape(N_ROW_BLOCKS, 1, MOE_ROWS)
    dst3 = dst.reshape(N_ROW_BLOCKS, 1, MOE_ROWS)
    return pl.pallas_call(
        _experts_kernel,
        grid_spec=grid_spec,
        out_shape=jax.ShapeDtypeStruct((N_SLOTS, PACKED), U32),
        compiler_params=_params(("arbitrary",)),
        name="experts",
    )(*items, tok3, tok3, dst3, dst3, hf, w_gate, w_up, w_down)


def _combine_kernel(wt_ref, base_ref, mod_ref, ys_ref, o_ref):
    w_tok = wt_ref[...].T
    acc_lo = acc_hi = None
    for kk in range(TOP_K):
        lo, hi = _unpack_row_halves(ys_ref[kk])
        w = w_tok[:, kk:kk + 1]
        acc_lo = w * lo if kk == 0 else acc_lo + w * lo
        acc_hi = w * hi if kk == 0 else acc_hi + w * hi
    o_ref[:, :PACKED] = base_ref[:, :PACKED] + mod_ref[0, 5:6, :PACKED] * acc_lo
    o_ref[:, PACKED:] = base_ref[:, PACKED:] + mod_ref[0, 5:6, PACKED:] * acc_hi


def _combine(w_t, base, mod3, ys):
    tm = COMB_TM
    rows_per_batch = SEQ // tm
    return pl.pallas_call(
        _combine_kernel,
        grid=(N_TOK // tm,),
        in_specs=[
            pl.BlockSpec((TOP_K, tm), lambda i: (0, i)),
            pl.BlockSpec((tm, D_MODEL), lambda i: (i, 0)),
            pl.BlockSpec((1, 6, D_MODEL), lambda i: (i // rows_per_batch, 0, 0)),
            pl.BlockSpec((TOP_K, tm, PACKED), lambda i: (0, i, 0)),
        ],
        out_specs=pl.BlockSpec((tm, D_MODEL), lambda i: (i, 0)),
        out_shape=jax.ShapeDtypeStruct((N_TOK, D_MODEL), F32),
        compiler_params=_params(("arbitrary",)),
        name="combine",
    )(w_t, base, mod3, ys.reshape(TOP_K, N_TOK, PACKED))


def _work_items(counts):
    rows = MOE_ROWS
    grp_end = jnp.cumsum(counts)
    grp_start = grp_end - counts
    first_blk = grp_start // rows
    last_blk = (grp_end - 1) // rows
    n_blk = jnp.where(counts > 0, last_blk - first_blk + 1, 0)
    item_end = jnp.cumsum(n_blk)
    item_start = item_end - n_blk
    n_items = item_end[-1]

    ids = jnp.arange(N_EXPERTS, dtype=I32)
    nonempty = (n_blk > 0).astype(I32)
    ordinal = jnp.cumsum(nonempty) - nonempty
    later = lax.cummin(jnp.where(nonempty > 0, ids, N_EXPERTS), axis=0, reverse=True)
    nxt_e = jnp.concatenate([later[1:], jnp.full((1,), N_EXPERTS, I32)])
    nxt_e = jnp.where(nxt_e >= N_EXPERTS, -1, nxt_e)
    nxt2_e = jnp.where(nxt_e >= 0, jnp.take(nxt_e, jnp.maximum(nxt_e, 0)), -1)

    i = jnp.minimum(jnp.arange(MAX_ITEMS, dtype=I32), n_items - 1)
    own = (item_start[None, :] <= i[:, None]) & (i[:, None] < item_end[None, :])
    pick = lambda v: jnp.sum(jnp.where(own, v[None, :], 0), axis=1).astype(I32)
    blk = pick(first_blk - item_start) + i
    lo = jnp.maximum(pick(grp_start), blk * rows) - blk * rows
    hi = jnp.minimum(pick(grp_end), (blk + 1) * rows) - blk * rows
    first = (pick(item_start) == i).astype(I32)
    items = (blk, pick(ids), lo, hi, first, pick(ordinal) % WEIGHT_SLOTS, pick(nxt_e), pick(nxt2_e),
             n_items.astype(I32).reshape(1))
    return grp_start.astype(I32), items


def kernel(x, c, w_ada, b_ada, norm1_g, w_in, q_norm_g, k_norm_g, conv_w, attn_out_g, conv_out_g,
           w_o, norm2_g, w_router, router_bias, w_gate, w_up, w_down, ws_gate, ws_up, ws_down):
    b, s, d = x.shape
    x2 = x.reshape(b * s, d)

    c8 = jnp.pad(c, ((0, 8 - b), (0, 0)))
    mod3 = _adaln(c8, w_ada, b_ada)[:b].reshape(b, 6, d)

    qk_gain = jnp.concatenate([jnp.tile(q_norm_g[0] * HEAD_DIM ** -0.5, ATTN_HEADS),
                               jnp.tile(k_norm_g[0], ATTN_HEADS)]).reshape(1, 2 * ATTN_WIDTH)
    proj = _in_proj(x2, mod3, norm1_g, w_in, qk_gain)

    a, y = _mixer(proj, conv_w, attn_out_g, conv_out_g)
    x1 = _out_proj(a, y, w_o, x2, mod3)

    w_rt = w_router[0].T
    bias_col = jnp.broadcast_to(router_bias[0][:, None], (N_EXPERTS, LANES))
    idx_t, w_t, pos_t, cnt = _route(x1, norm2_g, mod3, w_rt, bias_col)

    grp_start, items = _work_items(cnt[:, 0])
    start_col = jnp.broadcast_to(grp_start.astype(F32)[:, None], (N_EXPERTS, LANES))
    slot_tok, slot_dst = _invperm(start_col, idx_t, pos_t)
    hf, base = _shared(x1, norm2_g, mod3, ws_gate, ws_up, ws_down)
    ys = _experts(items, slot_tok, slot_dst, hf, w_gate, w_up, w_down)
    out = _combine(w_t, base, mod3, ys)
    return out.reshape(b, s, d)
```

```python
import functools

import jax
import jax.numpy as jnp
from jax import lax
from jax.experimental import pallas as pl
from jax.experimental.pallas import tpu as pltpu

F32 = jnp.float32
BF16 = jnp.bfloat16
I32 = jnp.int32
U32 = jnp.uint32

D_MODEL = 2048
BATCH = 4
SEQ = 2048
N_TOK = BATCH * SEQ
HEAD_DIM = 128
ATTN_WIDTH = 1024
ATTN_HEADS = 8
CONV_WIDTH = 1024
CONV_GROUPS = 8
IN_WIDTH = 6144
CONV_K = 3
MOBA_BLOCK = 256
MOBA_NB = SEQ // MOBA_BLOCK
MOBA_TOPK = 3
N_EXPERTS = 256
TOP_K = 8
N_GROUPS = 8
GROUP_SIZE = N_EXPERTS // N_GROUPS
TOPK_GROUPS = 4
EXPERT_DIM = 512
SHARED_DIM = 512
ROUTED_SCALE = 2.5
EPS = 1e-6

LANES = 128
SUBLANES = 8
MXU_COLS = 256
MASKED = -1e30
VMEM_LIMIT = 56 * 1024 * 1024

ADA_TN = 1024
PROJ_TM = 1024
PROJ_TN = 512
IN_TN = 1024
OUT_TM = 2048
ROUTE_TM = 256
DISP_TM = 256
MOE_ROWS = 128
WEIGHT_SLOTS = 3
N_SLOTS = N_TOK * TOP_K
N_ROW_BLOCKS = N_SLOTS // MOE_ROWS
MAX_ITEMS = N_ROW_BLOCKS + N_EXPERTS
COMB_TM = 256


def _dot(a, b):
    return jnp.dot(a, b, preferred_element_type=F32)


def _dot_nt(a, b):
    return lax.dot_general(a, b, (((1,), (1,)), ((), ())), preferred_element_type=F32)


def _params(sem):
    return pltpu.CompilerParams(dimension_semantics=sem, vmem_limit_bytes=VMEM_LIMIT)


PACKED = D_MODEL // 2


def _pack_row_halves(v):
    lo = pltpu.bitcast(v[:, :PACKED].astype(BF16).astype(F32), U32)
    hi = pltpu.bitcast(v[:, PACKED:].astype(BF16).astype(F32), U32)
    return hi | (lo >> 16)


def _unpack_row_halves(w):
    lo = pltpu.bitcast(w << 16, F32)
    hi = pltpu.bitcast(w & jnp.uint32(0xFFFF0000), F32)
    return lo, hi


def _adaln_kernel(c_ref, w_ref, b_ref, o_ref):
    c = c_ref[...]
    o_ref[...] = _dot(c * jax.nn.sigmoid(c), w_ref[...]) + b_ref[...]


def _adaln(c8, w_ada, b_ada):
    n = w_ada.shape[-1]
    return pl.pallas_call(
        _adaln_kernel,
        grid=(n // ADA_TN,),
        in_specs=[
            pl.BlockSpec((8, D_MODEL), lambda j: (0, 0)),
            pl.BlockSpec((None, D_MODEL, ADA_TN), lambda j: (0, 0, j)),
            pl.BlockSpec((1, ADA_TN), lambda j: (0, j)),
        ],
        out_specs=pl.BlockSpec((8, ADA_TN), lambda j: (0, j)),
        out_shape=jax.ShapeDtypeStruct((8, n), F32),
        compiler_params=_params(("arbitrary",)),
        name="adaln",
    )(c8, w_ada, b_ada)


def _in_proj_kernel(x_ref, mod_ref, g_ref, w_ref, qkg_ref, o_ref, h_ref):
    j = pl.program_id(1)

    @pl.when(j == 0)
    def _():
        xf = x_ref[...]
        ms = jnp.mean(xf * xf, axis=-1, keepdims=True)
        y = xf * lax.rsqrt(ms + EPS) * g_ref[...]
        h_ref[...] = (y * (1.0 + mod_ref[0, 1:2, :]) + mod_ref[0, 0:1, :]).astype(BF16)

    @pl.when(j < 2 * ATTN_WIDTH // IN_TN)
    def _():
        for pair in range(IN_TN // MXU_COLS):
            cols = slice(pair * MXU_COLS, (pair + 1) * MXU_COLS)
            acc = _dot(h_ref[...], w_ref[:, cols].astype(BF16))
            for hh in range(MXU_COLS // HEAD_DIM):
                a = acc[:, hh * HEAD_DIM:(hh + 1) * HEAD_DIM]
                sl = slice(pair * MXU_COLS + hh * HEAD_DIM, pair * MXU_COLS + (hh + 1) * HEAD_DIM)
                ms = jnp.mean(a * a, axis=-1, keepdims=True)
                o_ref[:, sl] = (a * lax.rsqrt(ms + EPS) * qkg_ref[:, sl]).astype(BF16)

    @pl.when(j >= 2 * ATTN_WIDTH // IN_TN)
    def _():
        for pair in range(IN_TN // MXU_COLS):
            cols = slice(pair * MXU_COLS, (pair + 1) * MXU_COLS)
            o_ref[:, cols] = _dot(h_ref[...], w_ref[:, cols].astype(BF16)).astype(BF16)


def _in_proj(x2, mod3, norm1_g, w_in, qk_gain):
    n_qk = 2 * ATTN_WIDTH // IN_TN
    rows_per_batch = SEQ // PROJ_TM
    return pl.pallas_call(
        _in_proj_kernel,
        grid=(N_TOK // PROJ_TM, IN_WIDTH // IN_TN),
        in_specs=[
            pl.BlockSpec((PROJ_TM, D_MODEL), lambda i, j: (i, 0)),
            pl.BlockSpec((1, 6, D_MODEL), lambda i, j: (i // rows_per_batch, 0, 0)),
            pl.BlockSpec((1, D_MODEL), lambda i, j: (0, 0)),
            pl.BlockSpec((None, D_MODEL, IN_TN), lambda i, j: (0, 0, j)),
            pl.BlockSpec((1, IN_TN), lambda i, j: (0, jnp.minimum(j, n_qk - 1))),
        ],
        out_specs=pl.BlockSpec((PROJ_TM, IN_TN), lambda i, j: (i, j)),
        out_shape=jax.ShapeDtypeStruct((N_TOK, IN_WIDTH), BF16),
        scratch_shapes=[pltpu.VMEM((PROJ_TM, D_MODEL), BF16)],
        compiler_params=_params(("arbitrary", "arbitrary")),
        name="in_proj",
    )(x2, mod3, norm1_g, w_in, qk_gain)


def _mixer_kernel(q_ref, k_ref, v_ref, u_ref, bg_ref, cg_ref, cw_ref, ag_ref, yg_ref,
                  a_ref, y_ref, s_ref):
    blk = MOBA_BLOCK
    k = k_ref[...]
    v = v_ref[...]

    km = jnp.mean(k.astype(F32).reshape(MOBA_NB, blk, HEAD_DIM), axis=1)
    km_hi = km.astype(BF16)
    km_lo = (km - km_hi.astype(F32)).astype(BF16)
    zpad = jnp.zeros((LANES - MOBA_NB, HEAD_DIM), BF16)
    km_hi = jnp.concatenate([km_hi, zpad], axis=0)
    km_lo = jnp.concatenate([km_lo, zpad], axis=0)

    lane = lax.broadcasted_iota(I32, (blk, LANES), 1)
    row = lax.broadcasted_iota(I32, (blk, blk), 0)
    col = lax.broadcasted_iota(I32, (blk, blk), 1)

    for i in range(MOBA_NB):
        qi = q_ref[i * blk:(i + 1) * blk, :]
        if i > 0:
            if i > MOBA_TOPK:
                g = _dot_nt(qi, km_hi) + _dot_nt(qi, km_lo)
                rank = jnp.zeros((blk, LANES), F32)
                for jp in range(i):
                    cj = g[:, jp:jp + 1]
                    tie = jnp.where(lane > jp, 1.0, 0.0)
                    rank = rank + jnp.where(cj > g, 1.0, jnp.where(cj == g, tie, 0.0))
                sel = jnp.where(rank < MOBA_TOPK, 1.0, 0.0)
            else:
                sel = jnp.ones((blk, LANES), F32)
            for j in range(i):
                s = _dot_nt(qi, k[j * blk:(j + 1) * blk])
                s_ref[:, j * blk:(j + 1) * blk] = jnp.where(sel[:, j:j + 1] > 0.5, s, MASKED)
        s = _dot_nt(qi, k[i * blk:(i + 1) * blk])
        s_ref[:, i * blk:(i + 1) * blk] = jnp.where(col <= row, s, MASKED)

        width = (i + 1) * blk
        sc = s_ref[:, :width]
        m = jnp.max(sc, axis=-1, keepdims=True)
        p = jnp.exp(sc - m)
        denom = jnp.sum(p, axis=-1, keepdims=True)
        o = _dot(p.astype(BF16), v[:width]) / denom
        ms = jnp.mean(o * o, axis=-1, keepdims=True)
        a_ref[i * blk:(i + 1) * blk, :] = (o * lax.rsqrt(ms + EPS) * ag_ref[...]).astype(BF16)

    z = cg_ref[...].astype(F32) * u_ref[...].astype(F32)
    t = lax.broadcasted_iota(I32, z.shape, 0)
    z1 = jnp.where(t >= 1, pltpu.roll(z, 1, 0), 0.0)
    z2 = jnp.where(t >= 2, pltpu.roll(z, 2, 0), 0.0)
    y = cw_ref[0:1, :] * z2 + cw_ref[1:2, :] * z1 + cw_ref[2:3, :] * z
    y = bg_ref[...].astype(F32) * y
    ms = jnp.mean(y * y, axis=-1, keepdims=True)
    y_ref[...] = (y * lax.rsqrt(ms + EPS) * yg_ref[...]).astype(BF16)


def _mixer(proj, conv_w, attn_out_g, conv_out_g):
    h = ATTN_HEADS

    def col(off):
        return pl.BlockSpec((SEQ, HEAD_DIM), lambda b, g: (b, off + g))

    out_spec = pl.BlockSpec((SEQ, HEAD_DIM), lambda b, g: (b, g))
    vec_spec = pl.BlockSpec((1, HEAD_DIM), lambda b, g: (0, g))
    return pl.pallas_call(
        _mixer_kernel,
        grid=(BATCH, h),
        in_specs=[col(0), col(h), col(2 * h), col(3 * h), col(4 * h), col(5 * h),
                  pl.BlockSpec((None, CONV_K, HEAD_DIM), lambda b, g: (0, 0, g)),
                  vec_spec, vec_spec],
        out_specs=[out_spec, out_spec],
        out_shape=[jax.ShapeDtypeStruct((N_TOK, ATTN_WIDTH), BF16),
                   jax.ShapeDtypeStruct((N_TOK, CONV_WIDTH), BF16)],
        scratch_shapes=[pltpu.VMEM((MOBA_BLOCK, SEQ), F32)],
        compiler_params=_params(("arbitrary", "arbitrary")),
        name="mixer",
    )(proj, proj, proj, proj, proj, proj, conv_w, attn_out_g, conv_out_g)


def _out_proj_kernel(a_ref, y_ref, wa_ref, wb_ref, x_ref, mod_ref, o_ref):
    mix = _dot(a_ref[...], wa_ref[...].astype(BF16)) + _dot(y_ref[...], wb_ref[...].astype(BF16))
    o_ref[...] = x_ref[...] + mod_ref[0, 2:3, :] * mix


def _out_proj(a, y, w_o, x2, mod3):
    tm = OUT_TM
    rows_per_batch = SEQ // tm
    return pl.pallas_call(
        _out_proj_kernel,
        grid=(N_TOK // tm, D_MODEL // PROJ_TN),
        in_specs=[
            pl.BlockSpec((tm, ATTN_WIDTH), lambda i, j: (i, 0)),
            pl.BlockSpec((tm, CONV_WIDTH), lambda i, j: (i, 0)),
            pl.BlockSpec((None, ATTN_WIDTH, PROJ_TN), lambda i, j: (0, 0, j)),
            pl.BlockSpec((None, CONV_WIDTH, PROJ_TN), lambda i, j: (0, 1, j)),
            pl.BlockSpec((tm, PROJ_TN), lambda i, j: (i, j)),
            pl.BlockSpec((1, 6, PROJ_TN), lambda i, j: (i // rows_per_batch, 0, j)),
        ],
        out_specs=pl.BlockSpec((tm, PROJ_TN), lambda i, j: (i, j)),
        out_shape=jax.ShapeDtypeStruct((N_TOK, D_MODEL), F32),
        compiler_params=_params(("arbitrary", "arbitrary")),
        name="out_proj",
    )(a, y, w_o, w_o, x2, mod3)


def _norm2_modulate(xf, g_ref, mod_ref):
    ms = jnp.mean(xf * xf, axis=-1, keepdims=True)
    return xf * lax.rsqrt(ms + EPS) * g_ref[...] * (1.0 + mod_ref[0, 4:5, :]) + mod_ref[0, 3:4, :]


def _route_kernel(x_ref, g_ref, mod_ref, wrt_ref, bias_ref,
                  idx_ref, wt_ref, pos_ref, cnt_ref, carry_ref):
    tm = ROUTE_TM
    ne = N_EXPERTS

    @pl.when(pl.program_id(0) == 0)
    def _():
        carry_ref[...] = jnp.zeros_like(carry_ref)

    hb = _norm2_modulate(x_ref[...], g_ref, mod_ref).astype(BF16)

    scores = jax.nn.sigmoid(_dot_nt(wrt_ref[...].astype(BF16), hb))
    choice = scores + bias_ref[:, 0:1]
    ninf = -jnp.inf

    gi = lax.broadcasted_iota(I32, (GROUP_SIZE, tm), 0)
    rows = []
    for g in range(N_GROUPS):
        blk = choice[g * GROUP_SIZE:(g + 1) * GROUP_SIZE, :]
        m1 = jnp.max(blk, axis=0, keepdims=True)
        i1 = jnp.min(jnp.where(blk == m1, gi, GROUP_SIZE), axis=0, keepdims=True)
        m2 = jnp.max(jnp.where(gi == i1, ninf, blk), axis=0, keepdims=True)
        rows.append(m1 + m2)
    gsc = jnp.concatenate(rows, axis=0)

    gidx = lax.broadcasted_iota(I32, (N_GROUPS, tm), 0)
    rank = jnp.zeros((N_GROUPS, tm), F32)
    for gp in range(N_GROUPS):
        r = gsc[gp:gp + 1, :]
        tie = jnp.where(gidx > gp, 1.0, 0.0)
        rank = rank + jnp.where(r > gsc, 1.0, jnp.where(r == gsc, tie, 0.0))
    gsel = jnp.where(rank < TOPK_GROUPS, 1.0, 0.0)
    esel = jnp.concatenate(
        [jnp.broadcast_to(gsel[g:g + 1, :], (GROUP_SIZE, tm)) for g in range(N_GROUPS)], axis=0)
    masked = jnp.where(esel > 0.5, choice, ninf)

    eidx = lax.broadcasted_iota(I32, (ne, tm), 0)
    idx_rows, w_rows = [], []
    onehot = jnp.zeros((ne, tm), F32)
    for _ in range(TOP_K):
        m = jnp.max(masked, axis=0, keepdims=True)
        sel = jnp.min(jnp.where(masked == m, eidx, ne), axis=0, keepdims=True)
        hit = eidx == sel
        idx_rows.append(sel)
        w_rows.append(jnp.sum(jnp.where(hit, scores, 0.0), axis=0, keepdims=True))
        masked = jnp.where(hit, ninf, masked)
        onehot = jnp.where(hit, 1.0, onehot)
    wsel = jnp.concatenate(w_rows, axis=0)
    idx_ref[...] = jnp.concatenate(idx_rows, axis=0)
    wt_ref[...] = wsel / jnp.sum(wsel, axis=0, keepdims=True) * ROUTED_SCALE

    ti = lax.broadcasted_iota(I32, (tm, tm), 0)
    tj = lax.broadcasted_iota(I32, (tm, tm), 1)
    upper = jnp.where(ti < tj, 1.0, 0.0).astype(BF16)
    before = _dot(onehot.astype(BF16), upper) + carry_ref[:, 0:1]
    pos_rows = [jnp.sum(jnp.where(eidx == idx_rows[kk], before, 0.0), axis=0, keepdims=True)
                for kk in range(TOP_K)]
    pos_ref[...] = jnp.concatenate(pos_rows, axis=0).astype(I32)
    carry_ref[...] = carry_ref[...] + jnp.sum(onehot, axis=1, keepdims=True)
    cnt_ref[...] = carry_ref[...].astype(I32)


def _route(x1, norm2_g, mod3, w_rt, bias_col):
    tm = ROUTE_TM
    rows_per_batch = SEQ // tm
    full = lambda shape: pl.BlockSpec(shape, lambda i: tuple(0 for _ in shape))
    tok_tile = pl.BlockSpec((TOP_K, tm), lambda i: (0, i))
    return pl.pallas_call(
        _route_kernel,
        grid=(N_TOK // tm,),
        in_specs=[
            pl.BlockSpec((tm, D_MODEL), lambda i: (i, 0)),
            full((1, D_MODEL)),
            pl.BlockSpec((1, 6, D_MODEL), lambda i: (i // rows_per_batch, 0, 0)),
            full((N_EXPERTS, D_MODEL)),
            full((N_EXPERTS, LANES)),
        ],
        out_specs=[tok_tile, tok_tile, tok_tile, full((N_EXPERTS, LANES))],
        out_shape=[
            jax.ShapeDtypeStruct((TOP_K, N_TOK), I32),
            jax.ShapeDtypeStruct((TOP_K, N_TOK), F32),
            jax.ShapeDtypeStruct((TOP_K, N_TOK), I32),
            jax.ShapeDtypeStruct((N_EXPERTS, LANES), I32),
        ],
        scratch_shapes=[pltpu.VMEM((N_EXPERTS, LANES), F32)],
        compiler_params=_params(("arbitrary",)),
        name="route",
    )(x1, norm2_g, mod3, w_rt, bias_col)


def _shared_kernel(x_ref, g_ref, mod_ref, wsg_ref, wsu_ref, wsd_ref, hf_ref, base_ref, wsg_bf, wsu_bf, wsd_bf):
    @pl.when(pl.program_id(0) == 0)
    def _():
        wsg_bf[...] = wsg_ref[...].astype(BF16)
        wsu_bf[...] = wsu_ref[...].astype(BF16)
        wsd_bf[...] = wsd_ref[...].astype(BF16)

    xf = x_ref[...]
    hf = _norm2_modulate(xf, g_ref, mod_ref)
    hf_ref[...] = _pack_row_halves(hf)

    hb = hf.astype(BF16)
    hg = _dot(hb, wsg_bf[...])
    hu = _dot(hb, wsu_bf[...])
    hid = (hg * jax.nn.sigmoid(hg)) * hu
    shared = _dot(hid.astype(BF16), wsd_bf[...])
    base_ref[...] = xf + mod_ref[0, 5:6, :] * shared


def _shared(x1, norm2_g, mod3, ws_gate, ws_up, ws_down):
    tm = DISP_TM
    rows_per_batch = SEQ // tm
    row_tile = pl.BlockSpec((tm, D_MODEL), lambda i: (i, 0))
    return pl.pallas_call(
        _shared_kernel,
        grid=(N_TOK // tm,),
        in_specs=[
            row_tile,
            pl.BlockSpec((1, D_MODEL), lambda i: (0, 0)),
            pl.BlockSpec((1, 6, D_MODEL), lambda i: (i // rows_per_batch, 0, 0)),
            pl.BlockSpec((None, D_MODEL, SHARED_DIM), lambda i: (0, 0, 0)),
            pl.BlockSpec((None, D_MODEL, SHARED_DIM), lambda i: (0, 0, 0)),
            pl.BlockSpec((None, SHARED_DIM, D_MODEL), lambda i: (0, 0, 0)),
        ],
        out_specs=[pl.BlockSpec((tm, PACKED), lambda i: (i, 0)), row_tile],
        out_shape=[jax.ShapeDtypeStruct((N_TOK, PACKED), U32),
                   jax.ShapeDtypeStruct((N_TOK, D_MODEL), F32)],
        scratch_shapes=[
            pltpu.VMEM((D_MODEL, SHARED_DIM), BF16),
            pltpu.VMEM((D_MODEL, SHARED_DIM), BF16),
            pltpu.VMEM((SHARED_DIM, D_MODEL), BF16),
        ],
        compiler_params=_params(("arbitrary",)),
        name="shared",
    )(x1, norm2_g, mod3, ws_gate, ws_up, ws_down)


DIGIT_BITS = 7
ROW_BITS = 7


def _invperm_kernel(start_ref, idx_ref, pos_ref, tok_ref, dst_ref, acc_ref):
    tm = ROUTE_TM
    step = pl.program_id(0)

    @pl.when(step == 0)
    def _():
        acc_ref[...] = jnp.zeros_like(acc_ref)

    eidx = lax.broadcasted_iota(I32, (N_EXPERTS, tm), 0)
    bidx = lax.broadcasted_iota(I32, (N_ROW_BLOCKS, tm), 0)
    ridx = lax.broadcasted_iota(I32, (MOE_ROWS, tm), 0)
    tok = step * tm + lax.broadcasted_iota(I32, (1, tm), 1)
    start = start_ref[:, 0:1]
    n_digits = -(-(TOP_K * N_TOK).bit_length() // DIGIT_BITS)

    acc = acc_ref[...]
    for kk in range(TOP_K):
        first = jnp.sum(jnp.where(eidx == idx_ref[kk:kk + 1, :], start, 0.0), axis=0, keepdims=True)
        slot = first.astype(I32) + pos_ref[kk:kk + 1, :]
        in_blk = bidx == (slot >> ROW_BITS)
        in_row = jnp.where(ridx == (slot & (MOE_ROWS - 1)), 1.0, 0.0).astype(BF16)
        code = kk * N_TOK + tok + 1
        for d in range(n_digits):
            digit = ((code >> (d * DIGIT_BITS)) & ((1 << DIGIT_BITS) - 1)).astype(F32)
            part = _dot_nt(jnp.where(in_blk, digit, 0.0).astype(BF16), in_row)
            acc = acc + float(1 << (d * DIGIT_BITS)) * part
    acc_ref[...] = acc
    dst = acc.astype(I32) - 1
    dst_ref[...] = dst
    tok_ref[...] = dst & (N_TOK - 1)


def _invperm(start_col, idx_t, pos_t):
    tm = ROUTE_TM
    tok_tile = pl.BlockSpec((TOP_K, tm), lambda i: (0, i))
    table = pl.BlockSpec((N_ROW_BLOCKS, MOE_ROWS), lambda i: (0, 0))
    table_shape = jax.ShapeDtypeStruct((N_ROW_BLOCKS, MOE_ROWS), I32)
    return pl.pallas_call(
        _invperm_kernel,
        grid=(N_TOK // tm,),
        in_specs=[pl.BlockSpec((N_EXPERTS, LANES), lambda i: (0, 0)), tok_tile, tok_tile],
        out_specs=[table, table],
        out_shape=[table_shape, table_shape],
        scratch_shapes=[pltpu.VMEM((N_ROW_BLOCKS, MOE_ROWS), F32)],
        compiler_params=_params(("arbitrary",)),
        name="invperm",
    )(start_col, idx_t, pos_t)


def _weight_copies(hbm_refs, expert, bufs, slot, sems):
    return [pltpu.make_async_copy(h.at[0, expert], b.at[slot], sems.at[slot, j])
            for j, (h, b) in enumerate(zip(hbm_refs, bufs))]


def _for_each_row(fn):
    def tile(r8, carry):
        r0 = pl.multiple_of(r8 * SUBLANES, SUBLANES)
        for j in range(SUBLANES):
            fn(r0 + j)
        return carry

    lax.fori_loop(0, MOE_ROWS // SUBLANES, tile, 0)


def _experts_kernel(blk_ref, exp_ref, lo_ref, hi_ref, first_ref, par_ref, nxt_ref, nxt2_ref, n_ref,
                    tok_ref, tok_next_ref, dst_ref, dst_prev_ref, hf_hbm, wg_hbm, wu_hbm, wd_hbm, ys_hbm,
                    x_buf, o_buf, wg_buf, wu_buf, wd_buf, wg_bf, wu_bf, wd_bf, sems, gsem, ssem):
    i = pl.program_id(0)
    hbm = (wg_hbm, wu_hbm, wd_hbm)
    bufs = (wg_buf, wu_buf, wd_buf)
    n_items = n_ref[0]
    active = i < n_items
    expert = exp_ref[i]
    slot = par_ref[i]
    blk = blk_ref[i]
    bslot = blk & 1
    lo = lo_ref[i]
    hi = hi_ref[i]

    def gather_copy(tok, r, s):
        return pltpu.make_async_copy(hf_hbm.at[pl.ds(tok, 1), :], x_buf.at[s, pl.ds(r, 1), :], gsem.at[s])

    def scatter_copy(r, dst, s):
        return pltpu.make_async_copy(o_buf.at[s, pl.ds(r, 1), :], ys_hbm.at[pl.ds(dst, 1), :], ssem.at[s])

    def start_gather(table, s):
        _for_each_row(lambda r: gather_copy(table[0, 0, r], r, s).start())

    def wait_gather(s):
        _for_each_row(lambda r: gather_copy(0, 0, s).wait())

    def wait_scatter(s):
        _for_each_row(lambda r: scatter_copy(0, 0, s).wait())

    @pl.when(i == 0)
    def _():
        for cp in _weight_copies(hbm, expert, bufs, 0, sems):
            cp.start(priority=1)

        @pl.when(nxt_ref[0] >= 0)
        def _():
            for cp in _weight_copies(hbm, nxt_ref[0], bufs, 1, sems):
                cp.start(priority=1)

        start_gather(tok_ref, bslot)

    first_of_block = jnp.logical_and(active, lo == 0)
    interior = jnp.logical_and(blk >= 1, blk + 1 < N_ROW_BLOCKS)

    @pl.when(first_of_block)
    def _():
        wait_gather(bslot)

        @pl.when(blk >= 2)
        def _():
            wait_scatter(bslot)

        @pl.when(blk == 0)
        def _():
            start_gather(tok_next_ref, 1 - bslot)

        @pl.when(blk == N_ROW_BLOCKS - 1)
        def _():
            _for_each_row(lambda r: scatter_copy(r, dst_prev_ref[0, 0, r], 1 - bslot).start())

    @pl.when(jnp.logical_and(active, first_ref[i] == 1))
    def _():
        for cp in _weight_copies(hbm, expert, bufs, slot, sems):
            cp.wait()
        ahead = nxt2_ref[i]

        @pl.when(ahead >= 0)
        def _():
            for cp in _weight_copies(hbm, ahead, bufs, jnp.where(slot == 0, WEIGHT_SLOTS - 1, slot - 1), sems):
                cp.start(priority=1)

        wg_bf[...] = wg_buf[slot].astype(BF16)
        wu_bf[...] = wu_buf[slot].astype(BF16)
        wd_bf[...] = wd_buf[slot].astype(BF16)

    def ffn(first_owner, s):
        x = jnp.concatenate(_unpack_row_halves(x_buf[s]), axis=1).astype(BF16)
        hg = _dot(x, wg_bf[...])
        hu = _dot(x, wu_bf[...])
        hid = (hg * jax.nn.sigmoid(hg)) * hu
        y = _pack_row_halves(_dot(hid.astype(BF16), wd_bf[...]))
        r = lax.broadcasted_iota(I32, (MOE_ROWS, 1), 0)
        mine = (r >= lo) & (r < hi)
        o_buf[s] = jnp.where(mine, y, jnp.uint32(0) if first_owner else o_buf[s])

    for s in range(2):
        in_slot = bslot == s

        @pl.when(jnp.logical_and(in_slot, jnp.logical_and(first_of_block, interior)))
        def _():
            for r in range(MOE_ROWS):
                scatter_copy(r, dst_prev_ref[0, 0, r], 1 - s).start()
                gather_copy(tok_next_ref[0, 0, r], r, 1 - s).start()
            ffn(True, s)

        @pl.when(jnp.logical_and(in_slot, jnp.logical_and(first_of_block, jnp.logical_not(interior))))
        def _():
            ffn(True, s)

        @pl.when(jnp.logical_and(in_slot, jnp.logical_and(active, lo > 0)))
        def _():
            ffn(False, s)

    @pl.when(i == n_items - 1)
    def _():
        _for_each_row(lambda r: scatter_copy(r, dst_ref[0, 0, r], bslot).start())
        wait_scatter(1 - bslot)
        wait_scatter(bslot)


def _experts(items, tok, dst, hf, w_gate, w_up, w_down):
    table = lambda step: pl.BlockSpec(
        (1, 1, MOE_ROWS), lambda i, blk, *_: (jnp.clip(blk[i] + step, 0, N_ROW_BLOCKS - 1), 0, 0),
        memory_space=pltpu.SMEM)
    hbm = pl.BlockSpec(memory_space=pl.ANY)
    grid_spec = pltpu.PrefetchScalarGridSpec(
        num_scalar_prefetch=len(items),
        grid=(MAX_ITEMS,),
        in_specs=[table(0), table(1), table(0), table(-1), hbm, hbm, hbm, hbm],
        out_specs=hbm,
        scratch_shapes=[
            pltpu.VMEM((2, MOE_ROWS, PACKED), U32),
            pltpu.VMEM((2, MOE_ROWS, PACKED), U32),
            pltpu.VMEM((WEIGHT_SLOTS, D_MODEL, EXPERT_DIM), F32),
            pltpu.VMEM((WEIGHT_SLOTS, D_MODEL, EXPERT_DIM), F32),
            pltpu.VMEM((WEIGHT_SLOTS, EXPERT_DIM, D_MODEL), F32),
            pltpu.VMEM((D_MODEL, EXPERT_DIM), BF16),
            pltpu.VMEM((D_MODEL, EXPERT_DIM), BF16),
            pltpu.VMEM((EXPERT_DIM, D_MODEL), BF16),
            pltpu.SemaphoreType.DMA((WEIGHT_SLOTS, 3)),
            pltpu.SemaphoreType.DMA((2,)),
            pltpu.SemaphoreType.DMA((2,)),
        ],
    )
    tok3 = tok.reshape(N_ROW_BLOCKS, 1, MOE_ROWS)
    dst3 = dst.reshape(N_ROW_BLOCKS, 1, MOE_ROWS)
    return pl.pallas_call(
        _experts_kernel,
        grid_spec=grid_spec,
        out_shape=jax.ShapeDtypeStruct((N_SLOTS, PACKED), U32),
        compiler_params=_params(("arbitrary",)),
        name="experts",
    )(*items, tok3, tok3, dst3, dst3, hf, w_gate, w_up, w_down)


def _combine_kernel(wt_ref, base_ref, mod_ref, ys_ref, o_ref):
    w_tok = wt_ref[...].T
    acc_lo = acc_hi = None
    for kk in range(TOP_K):
        lo, hi = _unpack_row_halves(ys_ref[kk])
        w = w_tok[:, kk:kk + 1]
        acc_lo = w * lo if kk == 0 else acc_lo + w * lo
        acc_hi = w * hi if kk == 0 else acc_hi + w * hi
    o_ref[:, :PACKED] = base_ref[:, :PACKED] + mod_ref[0, 5:6, :PACKED] * acc_lo
    o_ref[:, PACKED:] = base_ref[:, PACKED:] + mod_ref[0, 5:6, PACKED:] * acc_hi


def _combine(w_t, base, mod3, ys):
    tm = COMB_TM
    rows_per_batch = SEQ // tm
    return pl.pallas_call(
        _combine_kernel,
        grid=(N_TOK // tm,),
        in_specs=[
            pl.BlockSpec((TOP_K, tm), lambda i: (0, i)),
            pl.BlockSpec((tm, D_MODEL), lambda i: (i, 0)),
            pl.BlockSpec((1, 6, D_MODEL), lambda i: (i // rows_per_batch, 0, 0)),
            pl.BlockSpec((TOP_K, tm, PACKED), lambda i: (0, i, 0)),
        ],
        out_specs=pl.BlockSpec((tm, D_MODEL), lambda i: (i, 0)),
        out_shape=jax.ShapeDtypeStruct((N_TOK, D_MODEL), F32),
        compiler_params=_params(("arbitrary",)),
        name="combine",
    )(w_t, base, mod3, ys.reshape(TOP_K, N_TOK, PACKED))


def _work_items(counts):
    rows = MOE_ROWS
    grp_end = jnp.cumsum(counts)
    grp_start = grp_end - counts
    first_blk = grp_start // rows
    last_blk = (grp_end - 1) // rows
    n_blk = jnp.where(counts > 0, last_blk - first_blk + 1, 0)
    item_end = jnp.cumsum(n_blk)
    item_start = item_end - n_blk
    n_items = item_end[-1]

    ids = jnp.arange(N_EXPERTS, dtype=I32)
    nonempty = (n_blk > 0).astype(I32)
    ordinal = jnp.cumsum(nonempty) - nonempty
    later = lax.cummin(jnp.where(nonempty > 0, ids, N_EXPERTS), axis=0, reverse=True)
    nxt_e = jnp.concatenate([later[1:], jnp.full((1,), N_EXPERTS, I32)])
    nxt_e = jnp.where(nxt_e >= N_EXPERTS, -1, nxt_e)
    nxt2_e = jnp.where(nxt_e >= 0, jnp.take(nxt_e, jnp.maximum(nxt_e, 0)), -1)

    i = jnp.minimum(jnp.arange(MAX_ITEMS, dtype=I32), n_items - 1)
    own = (item_start[None, :] <= i[:, None]) & (i[:, None] < item_end[None, :])
    pick = lambda v: jnp.sum(jnp.where(own, v[None, :], 0), axis=1).astype(I32)
    blk = pick(first_blk - item_start) + i
    lo = jnp.maximum(pick(grp_start), blk * rows) - blk * rows
    hi = jnp.minimum(pick(grp_end), (blk + 1) * rows) - blk * rows
    first = (pick(item_start) == i).astype(I32)
    items = (blk, pick(ids), lo, hi, first, pick(ordinal) % WEIGHT_SLOTS, pick(nxt_e), pick(nxt2_e),
             n_items.astype(I32).reshape(1))
    return grp_start.astype(I32), items


def kernel(x, c, w_ada, b_ada, norm1_g, w_in, q_norm_g, k_norm_g, conv_w, attn_out_g, conv_out_g,
           w_o, norm2_g, w_router, router_bias, w_gate, w_up, w_down, ws_gate, ws_up, ws_down):
    b, s, d = x.shape
    x2 = x.reshape(b * s, d)

    c8 = jnp.pad(c, ((0, 8 - b), (0, 0)))
    mod3 = _adaln(c8, w_ada, b_ada)[:b].reshape(b, 6, d)

    qk_gain = jnp.concatenate([jnp.tile(q_norm_g[0] * HEAD_DIM ** -0.5, ATTN_HEADS),
                               jnp.tile(k_norm_g[0], ATTN_HEADS)]).reshape(1, 2 * ATTN_WIDTH)
    proj = _in_proj(x2, mod3, norm1_g, w_in, qk_gain)

    a, y = _mixer(proj, conv_w, attn_out_g, conv_out_g)
    x1 = _out_proj(a, y, w_o, x2, mod3)

    w_rt = w_router[0].T
    bias_col = jnp.broadcast_to(router_bias[0][:, None], (N_EXPERTS, LANES))
    idx_t, w_t, pos_t, cnt = _route(x1, norm2_g, mod3, w_rt, bias_col)

    grp_start, items = _work_items(cnt[:, 0])
    start_col = jnp.broadcast_to(grp_start.astype(F32)[:, None], (N_EXPERTS, LANES))
    slot_tok, slot_dst = _invperm(start_col, idx_t, pos_t)
    hf, base = _shared(x1, norm2_g, mod3, ws_gate, ws_up, ws_down)
    ys = _experts(items, slot_tok, slot_dst, hf, w_gate, w_up, w_down)
    out = _combine(w_t, base, mod3, ys)
    return out.reshape(b, s, d)
```

```python
import functools

import jax
import jax.numpy as jnp
from jax import lax
from jax.experimental import pallas as pl
from jax.experimental.pallas import tpu as pltpu

F32 = jnp.float32
BF16 = jnp.bfloat16
I32 = jnp.int32
U32 = jnp.uint32

D_MODEL = 2048
BATCH = 4
SEQ = 2048
N_TOK = BATCH * SEQ
HEAD_DIM = 128
ATTN_WIDTH = 1024
ATTN_HEADS = 8
CONV_WIDTH = 1024
CONV_GROUPS = 8
IN_WIDTH = 6144
CONV_K = 3
MOBA_BLOCK = 256
MOBA_NB = SEQ // MOBA_BLOCK
MOBA_TOPK = 3
N_EXPERTS = 256
TOP_K = 8
N_GROUPS = 8
GROUP_SIZE = N_EXPERTS // N_GROUPS
TOPK_GROUPS = 4
EXPERT_DIM = 512
SHARED_DIM = 512
ROUTED_SCALE = 2.5
EPS = 1e-6

LANES = 128
SUBLANES = 8
MXU_COLS = 256
MASKED = -1e30
VMEM_LIMIT = 56 * 1024 * 1024

ADA_TN = 1024
PROJ_TM = 1024
PROJ_TN = 512
IN_TN = 1024
OUT_TM = 2048
ROUTE_TM = 256
DISP_TM = 256
MOE_ROWS = 128
WEIGHT_SLOTS = 3
N_SLOTS = N_TOK * TOP_K
N_ROW_BLOCKS = N_SLOTS // MOE_ROWS
MAX_ITEMS = N_ROW_BLOCKS + N_EXPERTS
COMB_TM = 256


def _dot(a, b):
    return jnp.dot(a, b, preferred_element_type=F32)


def _dot_nt(a, b):
    return lax.dot_general(a, b, (((1,), (1,)), ((), ())), preferred_element_type=F32)


def _params(sem):
    return pltpu.CompilerParams(dimension_semantics=sem, vmem_limit_bytes=VMEM_LIMIT)


PACKED = D_MODEL // 2


def _pack_row_halves(v):
    lo = pltpu.bitcast(v[:, :PACKED].astype(BF16).astype(F32), U32)
    hi = pltpu.bitcast(v[:, PACKED:].astype(BF16).astype(F32), U32)
    return hi | (lo >> 16)


def _unpack_row_halves(w):
    lo = pltpu.bitcast(w << 16, F32)
    hi = pltpu.bitcast(w & jnp.uint32(0xFFFF0000), F32)
    return lo, hi


def _adaln_kernel(c_ref, w_ref, b_ref, o_ref):
    c = c_ref[...]
    o_ref[...] = _dot(c * jax.nn.sigmoid(c), w_ref[...]) + b_ref[...]


def _adaln(c8, w_ada, b_ada):
    n = w_ada.shape[-1]
    return pl.pallas_call(
        _adaln_kernel,
        grid=(n // ADA_TN,),
        in_specs=[
            pl.BlockSpec((8, D_MODEL), lambda j: (0, 0)),
            pl.BlockSpec((None, D_MODEL, ADA_TN), lambda j: (0, 0, j)),
            pl.BlockSpec((1, ADA_TN), lambda j: (0, j)),
        ],
        out_specs=pl.BlockSpec((8, ADA_TN), lambda j: (0, j)),
        out_shape=jax.ShapeDtypeStruct((8, n), F32),
        compiler_params=_params(("arbitrary",)),
        name="adaln",
    )(c8, w_ada, b_ada)


def _in_proj_kernel(x_ref, mod_ref, g_ref, w_ref, qkg_ref, o_ref, h_ref):
    j = pl.program_id(1)

    @pl.when(j == 0)
    def _():
        xf = x_ref[...]
        ms = jnp.mean(xf * xf, axis=-1, keepdims=True)
        y = xf * lax.rsqrt(ms + EPS) * g_ref[...]
        h_ref[...] = (y * (1.0 + mod_ref[0, 1:2, :]) + mod_ref[0, 0:1, :]).astype(BF16)

    @pl.when(j < 2 * ATTN_WIDTH // IN_TN)
    def _():
        for pair in range(IN_TN // MXU_COLS):
            cols = slice(pair * MXU_COLS, (pair + 1) * MXU_COLS)
            acc = _dot(h_ref[...], w_ref[:, cols].astype(BF16))
            for hh in range(MXU_COLS // HEAD_DIM):
                a = acc[:, hh * HEAD_DIM:(hh + 1) * HEAD_DIM]
                sl = slice(pair * MXU_COLS + hh * HEAD_DIM, pair * MXU_COLS + (hh + 1) * HEAD_DIM)
                ms = jnp.mean(a * a, axis=-1, keepdims=True)
                o_ref[:, sl] = (a * lax.rsqrt(ms + EPS) * qkg_ref[:, sl]).astype(BF16)

    @pl.when(j >= 2 * ATTN_WIDTH // IN_TN)
    def _():
        for pair in range(IN_TN // MXU_COLS):
            cols = slice(pair * MXU_COLS, (pair + 1) * MXU_COLS)
            o_ref[:, cols] = _dot(h_ref[...], w_ref[:, cols].astype(BF16)).astype(BF16)


def _in_proj(x2, mod3, norm1_g, w_in, qk_gain):
    n_qk = 2 * ATTN_WIDTH // IN_TN
    rows_per_batch = SEQ // PROJ_TM
    return pl.pallas_call(
        _in_proj_kernel,
        grid=(N_TOK // PROJ_TM, IN_WIDTH // IN_TN),
        in_specs=[
            pl.BlockSpec((PROJ_TM, D_MODEL), lambda i, j: (i, 0)),
            pl.BlockSpec((1, 6, D_MODEL), lambda i, j: (i // rows_per_batch, 0, 0)),
            pl.BlockSpec((1, D_MODEL), lambda i, j: (0, 0)),
            pl.BlockSpec((None, D_MODEL, IN_TN), lambda i, j: (0, 0, j)),
            pl.BlockSpec((1, IN_TN), lambda i, j: (0, jnp.minimum(j, n_qk - 1))),
        ],
        out_specs=pl.BlockSpec((PROJ_TM, IN_TN), lambda i, j: (i, j)),
        out_shape=jax.ShapeDtypeStruct((N_TOK, IN_WIDTH), BF16),
        scratch_shapes=[pltpu.VMEM((PROJ_TM, D_MODEL), BF16)],
        compiler_params=_params(("arbitrary", "arbitrary")),
        name="in_proj",
    )(x2, mod3, norm1_g, w_in, qk_gain)


def _mixer_kernel(q_ref, k_ref, v_ref, u_ref, bg_ref, cg_ref, cw_ref, ag_ref, yg_ref,
                  a_ref, y_ref, s_ref):
    blk = MOBA_BLOCK
    k = k_ref[...]
    v = v_ref[...]

    km = jnp.mean(k.astype(F32).reshape(MOBA_NB, blk, HEAD_DIM), axis=1)
    km_hi = km.astype(BF16)
    km_lo = (km - km_hi.astype(F32)).astype(BF16)
    zpad = jnp.zeros((LANES - MOBA_NB, HEAD_DIM), BF16)
    km_hi = jnp.concatenate([km_hi, zpad], axis=0)
    km_lo = jnp.concatenate([km_lo, zpad], axis=0)

    lane = lax.broadcasted_iota(I32, (blk, LANES), 1)
    row = lax.broadcasted_iota(I32, (blk, blk), 0)
    col = lax.broadcasted_iota(I32, (blk, blk), 1)

    for i in range(MOBA_NB):
        qi = q_ref[i * blk:(i + 1) * blk, :]
        if i > 0:
            if i > MOBA_TOPK:
                g = _dot_nt(qi, km_hi) + _dot_nt(qi, km_lo)
                rank = jnp.zeros((blk, LANES), F32)
                for jp in range(i):
                    cj = g[:, jp:jp + 1]
                    tie = jnp.where(lane > jp, 1.0, 0.0)
                    rank = rank + jnp.where(cj > g, 1.0, jnp.where(cj == g, tie, 0.0))
                sel = jnp.where(rank < MOBA_TOPK, 1.0, 0.0)
            else:
                sel = jnp.ones((blk, LANES), F32)
            for j in range(i):
                s = _dot_nt(qi, k[j * blk:(j + 1) * blk])
                s_ref[:, j * blk:(j + 1) * blk] = jnp.where(sel[:, j:j + 1] > 0.5, s, MASKED)
        s = _dot_nt(qi, k[i * blk:(i + 1) * blk])
        s_ref[:, i * blk:(i + 1) * blk] = jnp.where(col <= row, s, MASKED)

        width = (i + 1) * blk
        sc = s_ref[:, :width]
        m = jnp.max(sc, axis=-1, keepdims=True)
        p = jnp.exp(sc - m)
        denom = jnp.sum(p, axis=-1, keepdims=True)
        o = _dot(p.astype(BF16), v[:width]) / denom
        ms = jnp.mean(o * o, axis=-1, keepdims=True)
        a_ref[i * blk:(i + 1) * blk, :] = (o * lax.rsqrt(ms + EPS) * ag_ref[...]).astype(BF16)

    z = cg_ref[...].astype(F32) * u_ref[...].astype(F32)
    t = lax.broadcasted_iota(I32, z.shape, 0)
    z1 = jnp.where(t >= 1, pltpu.roll(z, 1, 0), 0.0)
    z2 = jnp.where(t >= 2, pltpu.roll(z, 2, 0), 0.0)
    y = cw_ref[0:1, :] * z2 + cw_ref[1:2, :] * z1 + cw_ref[2:3, :] * z
    y = bg_ref[...].astype(F32) * y
    ms = jnp.mean(y * y, axis=-1, keepdims=True)
    y_ref[...] = (y * lax.rsqrt(ms + EPS) * yg_ref[...]).astype(BF16)


def _mixer(proj, conv_w, attn_out_g, conv_out_g):
    h = ATTN_HEADS

    def col(off):
        return pl.BlockSpec((SEQ, HEAD_DIM), lambda b, g: (b, off + g))

    out_spec = pl.BlockSpec((SEQ, HEAD_DIM), lambda b, g: (b, g))
    vec_spec = pl.BlockSpec((1, HEAD_DIM), lambda b, g: (0, g))
    return pl.pallas_call(
        _mixer_kernel,
        grid=(BATCH, h),
        in_specs=[col(0), col(h), col(2 * h), col(3 * h), col(4 * h), col(5 * h),
                  pl.BlockSpec((None, CONV_K, HEAD_DIM), lambda b, g: (0, 0, g)),
                  vec_spec, vec_spec],
        out_specs=[out_spec, out_spec],
        out_shape=[jax.ShapeDtypeStruct((N_TOK, ATTN_WIDTH), BF16),
                   jax.ShapeDtypeStruct((N_TOK, CONV_WIDTH), BF16)],
        scratch_shapes=[pltpu.VMEM((MOBA_BLOCK, SEQ), F32)],
        compiler_params=_params(("arbitrary", "arbitrary")),
        name="mixer",
    )(proj, proj, proj, proj, proj, proj, conv_w, attn_out_g, conv_out_g)


def _out_proj_kernel(a_ref, y_ref, wa_ref, wb_ref, x_ref, mod_ref, o_ref):
    mix = _dot(a_ref[...], wa_ref[...].astype(BF16)) + _dot(y_ref[...], wb_ref[...].astype(BF16))
    o_ref[...] = x_ref[...] + mod_ref[0, 2:3, :] * mix


def _out_proj(a, y, w_o, x2, mod3):
    tm = OUT_TM
    rows_per_batch = SEQ // tm
    return pl.pallas_call(
        _out_proj_kernel,
        grid=(N_TOK // tm, D_MODEL // PROJ_TN),
        in_specs=[
            pl.BlockSpec((tm, ATTN_WIDTH), lambda i, j: (i, 0)),
            pl.BlockSpec((tm, CONV_WIDTH), lambda i, j: (i, 0)),
            pl.BlockSpec((None, ATTN_WIDTH, PROJ_TN), lambda i, j: (0, 0, j)),
            pl.BlockSpec((None, CONV_WIDTH, PROJ_TN), lambda i, j: (0, 1, j)),
            pl.BlockSpec((tm, PROJ_TN), lambda i, j: (i, j)),
            pl.BlockSpec((1, 6, PROJ_TN), lambda i, j: (i // rows_per_batch, 0, j)),
        ],
        out_specs=pl.BlockSpec((tm, PROJ_TN), lambda i, j: (i, j)),
        out_shape=jax.ShapeDtypeStruct((N_TOK, D_MODEL), F32),
        compiler_params=_params(("arbitrary", "arbitrary")),
        name="out_proj",
    )(a, y, w_o, w_o, x2, mod3)


def _norm2_modulate(xf, g_ref, mod_ref):
    ms = jnp.mean(xf * xf, axis=-1, keepdims=True)
    return xf * lax.rsqrt(ms + EPS) * g_ref[...] * (1.0 + mod_ref[0, 4:5, :]) + mod_ref[0, 3:4, :]


def _route_kernel(x_ref, g_ref, mod_ref, wrt_ref, bias_ref,
                  idx_ref, wt_ref, pos_ref, cnt_ref, carry_ref):
    tm = ROUTE_TM
    ne = N_EXPERTS

    @pl.when(pl.program_id(0) == 0)
    def _():
        carry_ref[...] = jnp.zeros_like(carry_ref)

    hb = _norm2_modulate(x_ref[...], g_ref, mod_ref).astype(BF16)

    scores = jax.nn.sigmoid(_dot_nt(wrt_ref[...].astype(BF16), hb))
    choice = scores + bias_ref[:, 0:1]
    ninf = -jnp.inf

    gi = lax.broadcasted_iota(I32, (GROUP_SIZE, tm), 0)
    rows = []
    for g in range(N_GROUPS):
        blk = choice[g * GROUP_SIZE:(g + 1) * GROUP_SIZE, :]
        m1 = jnp.max(blk, axis=0, keepdims=True)
        i1 = jnp.min(jnp.where(blk == m1, gi, GROUP_SIZE), axis=0, keepdims=True)
        m2 = jnp.max(jnp.where(gi == i1, ninf, blk), axis=0, keepdims=True)
        rows.append(m1 + m2)
    gsc = jnp.concatenate(rows, axis=0)

    gidx = lax.broadcasted_iota(I32, (N_GROUPS, tm), 0)
    rank = jnp.zeros((N_GROUPS, tm), F32)
    for gp in range(N_GROUPS):
        r = gsc[gp:gp + 1, :]
        tie = jnp.where(gidx > gp, 1.0, 0.0)
        rank = rank + jnp.where(r > gsc, 1.0, jnp.where(r == gsc, tie, 0.0))
    gsel = jnp.where(rank < TOPK_GROUPS, 1.0, 0.0)
    esel = jnp.concatenate(
        [jnp.broadcast_to(gsel[g:g + 1, :], (GROUP_SIZE, tm)) for g in range(N_GROUPS)], axis=0)
    masked = jnp.where(esel > 0.5, choice, ninf)

    eidx = lax.broadcasted_iota(I32, (ne, tm), 0)
    idx_rows, w_rows = [], []
    onehot = jnp.zeros((ne, tm), F32)
    for _ in range(TOP_K):
        m = jnp.max(masked, axis=0, keepdims=True)
        sel = jnp.min(jnp.where(masked == m, eidx, ne), axis=0, keepdims=True)
        hit = eidx == sel
        idx_rows.append(sel)
        w_rows.append(jnp.sum(jnp.where(hit, scores, 0.0), axis=0, keepdims=True))
        masked = jnp.where(hit, ninf, masked)
        onehot = jnp.where(hit, 1.0, onehot)
    wsel = jnp.concatenate(w_rows, axis=0)
    idx_ref[...] = jnp.concatenate(idx_rows, axis=0)
    wt_ref[...] = wsel / jnp.sum(wsel, axis=0, keepdims=True) * ROUTED_SCALE

    ti = lax.broadcasted_iota(I32, (tm, tm), 0)
    tj = lax.broadcasted_iota(I32, (tm, tm), 1)
    upper = jnp.where(ti < tj, 1.0, 0.0).astype(BF16)
    before = _dot(onehot.astype(BF16), upper) + carry_ref[:, 0:1]
    pos_rows = [jnp.sum(jnp.where(eidx == idx_rows[kk], before, 0.0), axis=0, keepdims=True)
                for kk in range(TOP_K)]
    pos_ref[...] = jnp.concatenate(pos_rows, axis=0).astype(I32)
    carry_ref[...] = carry_ref[...] + jnp.sum(onehot, axis=1, keepdims=True)
    cnt_ref[...] = carry_ref[...].astype(I32)


def _route(x1, norm2_g, mod3, w_rt, bias_col):
    tm = ROUTE_TM
    rows_per_batch = SEQ // tm
    full = lambda shape: pl.BlockSpec(shape, lambda i: tuple(0 for _ in shape))
    tok_tile = pl.BlockSpec((TOP_K, tm), lambda i: (0, i))
    return pl.pallas_call(
        _route_kernel,
        grid=(N_TOK // tm,),
        in_specs=[
            pl.BlockSpec((tm, D_MODEL), lambda i: (i, 0)),
            full((1, D_MODEL)),
            pl.BlockSpec((1, 6, D_MODEL), lambda i: (i // rows_per_batch, 0, 0)),
            full((N_EXPERTS, D_MODEL)),
            full((N_EXPERTS, LANES)),
        ],
        out_specs=[tok_tile, tok_tile, tok_tile, full((N_EXPERTS, LANES))],
        out_shape=[
            jax.ShapeDtypeStruct((TOP_K, N_TOK), I32),
            jax.ShapeDtypeStruct((TOP_K, N_TOK), F32),
            jax.ShapeDtypeStruct((TOP_K, N_TOK), I32),
            jax.ShapeDtypeStruct((N_EXPERTS, LANES), I32),
        ],
        scratch_shapes=[pltpu.VMEM((N_EXPERTS, LANES), F32)],
        compiler_params=_params(("arbitrary",)),
        name="route",
    )(x1, norm2_g, mod3, w_rt, bias_col)


def _shared_kernel(x_ref, g_ref, mod_ref, wsg_ref, wsu_ref, wsd_ref, hf_ref, base_ref, wsg_bf, wsu_bf, wsd_bf):
    @pl.when(pl.program_id(0) == 0)
    def _():
        wsg_bf[...] = wsg_ref[...].astype(BF16)
        wsu_bf[...] = wsu_ref[...].astype(BF16)
        wsd_bf[...] = wsd_ref[...].astype(BF16)

    xf = x_ref[...]
    hf = _norm2_modulate(xf, g_ref, mod_ref)
    hf_ref[...] = _pack_row_halves(hf)

    hb = hf.astype(BF16)
    hg = _dot(hb, wsg_bf[...])
    hu = _dot(hb, wsu_bf[...])
    hid = (hg * jax.nn.sigmoid(hg)) * hu
    shared = _dot(hid.astype(BF16), wsd_bf[...])
    base_ref[...] = xf + mod_ref[0, 5:6, :] * shared


def _shared(x1, norm2_g, mod3, ws_gate, ws_up, ws_down):
    tm = DISP_TM
    rows_per_batch = SEQ // tm
    row_tile = pl.BlockSpec((tm, D_MODEL), lambda i: (i, 0))
    return pl.pallas_call(
        _shared_kernel,
        grid=(N_TOK // tm,),
        in_specs=[
            row_tile,
            pl.BlockSpec((1, D_MODEL), lambda i: (0, 0)),
            pl.BlockSpec((1, 6, D_MODEL), lambda i: (i // rows_per_batch, 0, 0)),
            pl.BlockSpec((None, D_MODEL, SHARED_DIM), lambda i: (0, 0, 0)),
            pl.BlockSpec((None, D_MODEL, SHARED_DIM), lambda i: (0, 0, 0)),
            pl.BlockSpec((None, SHARED_DIM, D_MODEL), lambda i: (0, 0, 0)),
        ],
        out_specs=[pl.BlockSpec((tm, PACKED), lambda i: (i, 0)), row_tile],
        out_shape=[jax.ShapeDtypeStruct((N_TOK, PACKED), U32),
                   jax.ShapeDtypeStruct((N_TOK, D_MODEL), F32)],
        scratch_shapes=[
            pltpu.VMEM((D_MODEL, SHARED_DIM), BF16),
            pltpu.VMEM((D_MODEL, SHARED_DIM), BF16),
            pltpu.VMEM((SHARED_DIM, D_MODEL), BF16),
        ],
        compiler_params=_params(("arbitrary",)),
        name="shared",
    )(x1, norm2_g, mod3, ws_gate, ws_up, ws_down)


DIGIT_BITS = 8
ROW_BITS = 7


def _invperm_kernel(start_ref, idx_ref, pos_ref, tok_ref, dst_ref, acc_ref):
    tm = ROUTE_TM
    step = pl.program_id(0)

    @pl.when(step == 0)
    def _():
        acc_ref[...] = jnp.zeros_like(acc_ref)

    eidx = lax.broadcasted_iota(I32, (N_EXPERTS, tm), 0)
    bidx = lax.broadcasted_iota(I32, (N_ROW_BLOCKS, tm), 0)
    ridx = lax.broadcasted_iota(I32, (MOE_ROWS, tm), 0)
    tok = step * tm + lax.broadcasted_iota(I32, (1, tm), 1)
    start = start_ref[:, 0:1]
    n_digits = -(-(TOP_K * N_TOK - 1).bit_length() // DIGIT_BITS)

    acc = acc_ref[...]
    for kk in range(TOP_K):
        first = jnp.sum(jnp.where(eidx == idx_ref[kk:kk + 1, :], start, 0.0), axis=0, keepdims=True)
        slot = first.astype(I32) + pos_ref[kk:kk + 1, :]
        in_blk = bidx == (slot >> ROW_BITS)
        in_row = jnp.where(ridx == (slot & (MOE_ROWS - 1)), 1.0, 0.0).astype(BF16)
        code = kk * N_TOK + tok
        for d in range(n_digits):
            digit = ((code >> (d * DIGIT_BITS)) & ((1 << DIGIT_BITS) - 1)).astype(F32)
            part = _dot_nt(jnp.where(in_blk, digit, 0.0).astype(BF16), in_row)
            acc = acc + float(1 << (d * DIGIT_BITS)) * part
    acc_ref[...] = acc
    dst = acc.astype(I32)
    dst_ref[...] = dst
    tok_ref[...] = dst & (N_TOK - 1)


def _invperm(start_col, idx_t, pos_t):
    tm = ROUTE_TM
    tok_tile = pl.BlockSpec((TOP_K, tm), lambda i: (0, i))
    table = pl.BlockSpec((N_ROW_BLOCKS, MOE_ROWS), lambda i: (0, 0))
    table_shape = jax.ShapeDtypeStruct((N_ROW_BLOCKS, MOE_ROWS), I32)
    return pl.pallas_call(
        _invperm_kernel,
        grid=(N_TOK // tm,),
        in_specs=[pl.BlockSpec((N_EXPERTS, LANES), lambda i: (0, 0)), tok_tile, tok_tile],
        out_specs=[table, table],
        out_shape=[table_shape, table_shape],
        scratch_shapes=[pltpu.VMEM((N_ROW_BLOCKS, MOE_ROWS), F32)],
        compiler_params=_params(("arbitrary",)),
        name="invperm",
    )(start_col, idx_t, pos_t)


def _weight_copies(hbm_refs, expert, bufs, slot, sems):
    return [pltpu.make_async_copy(h.at[0, expert], b.at[slot], sems.at[slot, j])
            for j, (h, b) in enumerate(zip(hbm_refs, bufs))]


def _for_each_row(fn):
    def tile(r8, carry):
        r0 = pl.multiple_of(r8 * SUBLANES, SUBLANES)
        for j in range(SUBLANES):
            fn(r0 + j)
        return carry

    lax.fori_loop(0, MOE_ROWS // SUBLANES, tile, 0)


def _experts_kernel(blk_ref, exp_ref, lo_ref, hi_ref, first_ref, par_ref, nxt_ref, nxt2_ref, n_ref,
                    tok_ref, tok_next_ref, dst_ref, dst_prev_ref, hf_hbm, wg_hbm, wu_hbm, wd_hbm, ys_hbm,
                    x_buf, o_buf, wg_buf, wu_buf, wd_buf, wg_bf, wu_bf, wd_bf, sems, gsem, ssem):
    i = pl.program_id(0)
    hbm = (wg_hbm, wu_hbm, wd_hbm)
    bufs = (wg_buf, wu_buf, wd_buf)
    n_items = n_ref[0]
    active = i < n_items
    expert = exp_ref[i]
    slot = par_ref[i]
    blk = blk_ref[i]
    bslot = blk & 1
    lo = lo_ref[i]
    hi = hi_ref[i]

    def gather_copy(tok, r, s):
        return pltpu.make_async_copy(hf_hbm.at[pl.ds(tok, 1), :], x_buf.at[s, pl.ds(r, 1), :], gsem.at[s])

    def scatter_copy(r, dst, s):
        return pltpu.make_async_copy(o_buf.at[s, pl.ds(r, 1), :], ys_hbm.at[pl.ds(dst, 1), :], ssem.at[s])

    def start_gather(table, s):
        _for_each_row(lambda r: gather_copy(table[0, 0, r], r, s).start())

    def wait_gather(s):
        _for_each_row(lambda r: gather_copy(0, 0, s).wait())

    def wait_scatter(s):
        _for_each_row(lambda r: scatter_copy(0, 0, s).wait())

    @pl.when(i == 0)
    def _():
        for cp in _weight_copies(hbm, expert, bufs, 0, sems):
            cp.start(priority=1)

        @pl.when(nxt_ref[0] >= 0)
        def _():
            for cp in _weight_copies(hbm, nxt_ref[0], bufs, 1, sems):
                cp.start(priority=1)

        start_gather(tok_ref, bslot)

    first_of_block = jnp.logical_and(active, lo == 0)
    interior = jnp.logical_and(blk >= 1, blk + 1 < N_ROW_BLOCKS)

    @pl.when(first_of_block)
    def _():
        wait_gather(bslot)

        @pl.when(blk >= 2)
        def _():
            wait_scatter(bslot)

        @pl.when(blk == 0)
        def _():
            start_gather(tok_next_ref, 1 - bslot)

        @pl.when(blk == N_ROW_BLOCKS - 1)
        def _():
            _for_each_row(lambda r: scatter_copy(r, dst_prev_ref[0, 0, r], 1 - bslot).start())

    @pl.when(jnp.logical_and(active, first_ref[i] == 1))
    def _():
        for cp in _weight_copies(hbm, expert, bufs, slot, sems):
            cp.wait()
        ahead = nxt2_ref[i]

        @pl.when(ahead >= 0)
        def _():
            for cp in _weight_copies(hbm, ahead, bufs, jnp.where(slot == 0, WEIGHT_SLOTS - 1, slot - 1), sems):
                cp.start(priority=1)

        wg_bf[...] = wg_buf[slot].astype(BF16)
        wu_bf[...] = wu_buf[slot].astype(BF16)
        wd_bf[...] = wd_buf[slot].astype(BF16)

    def ffn(first_owner, s):
        x = jnp.concatenate(_unpack_row_halves(x_buf[s]), axis=1).astype(BF16)
        hg = _dot(x, wg_bf[...])
        hu = _dot(x, wu_bf[...])
        hid = (hg * jax.nn.sigmoid(hg)) * hu
        y = _pack_row_halves(_dot(hid.astype(BF16), wd_bf[...]))
        r = lax.broadcasted_iota(I32, (MOE_ROWS, 1), 0)
        mine = (r >= lo) & (r < hi)
        o_buf[s] = jnp.where(mine, y, jnp.uint32(0) if first_owner else o_buf[s])

    for s in range(2):
        in_slot = bslot == s

        @pl.when(jnp.logical_and(in_slot, jnp.logical_and(first_of_block, interior)))
        def _():
            for r in range(MOE_ROWS):
                scatter_copy(r, dst_prev_ref[0, 0, r], 1 - s).start()
                gather_copy(tok_next_ref[0, 0, r], r, 1 - s).start()
            ffn(True, s)

        @pl.when(jnp.logical_and(in_slot, jnp.logical_and(first_of_block, jnp.logical_not(interior))))
        def _():
            ffn(True, s)

        @pl.when(jnp.logical_and(in_slot, jnp.logical_and(active, lo > 0)))
        def _():
            ffn(False, s)

    @pl.when(i == n_items - 1)
    def _():
        _for_each_row(lambda r: scatter_copy(r, dst_ref[0, 0, r], bslot).start())
        wait_scatter(1 - bslot)
        wait_scatter(bslot)


def _experts(items, tok, dst, hf, w_gate, w_up, w_down):
    table = lambda step: pl.BlockSpec(
        (1, 1, MOE_ROWS), lambda i, blk, *_: (jnp.clip(blk[i] + step, 0, N_ROW_BLOCKS - 1), 0, 0),
        memory_space=pltpu.SMEM)
    hbm = pl.BlockSpec(memory_space=pl.ANY)
    grid_spec = pltpu.PrefetchScalarGridSpec(
        num_scalar_prefetch=len(items),
        grid=(MAX_ITEMS,),
        in_specs=[table(0), table(1), table(0), table(-1), hbm, hbm, hbm, hbm],
        out_specs=hbm,
        scratch_shapes=[
            pltpu.VMEM((2, MOE_ROWS, PACKED), U32),
            pltpu.VMEM((2, MOE_ROWS, PACKED), U32),
            pltpu.VMEM((WEIGHT_SLOTS, D_MODEL, EXPERT_DIM), F32),
            pltpu.VMEM((WEIGHT_SLOTS, D_MODEL, EXPERT_DIM), F32),
            pltpu.VMEM((WEIGHT_SLOTS, EXPERT_DIM, D_MODEL), F32),
            pltpu.VMEM((D_MODEL, EXPERT_DIM), BF16),
            pltpu.VMEM((D_MODEL, EXPERT_DIM), BF16),
            pltpu.VMEM((EXPERT_DIM, D_MODEL), BF16),
            pltpu.SemaphoreType.DMA((WEIGHT_SLOTS, 3)),
            pltpu.SemaphoreType.DMA((2,)),
            pltpu.SemaphoreType.DMA((2,)),
        ],
    )
    tok3 = tok.reshape(N_ROW_BLOCKS, 1, MOE_ROWS)
    dst3 = dst.reshape(N_ROW_BLOCKS, 1, MOE_ROWS)
    return pl.pallas_call(
        _experts_kernel,
        grid_spec=grid_spec,
        out_shape=jax.ShapeDtypeStruct((N_SLOTS, PACKED), U32),
        compiler_params=_params(("arbitrary",)),
        name="experts",
    )(*items, tok3, tok3, dst3, dst3, hf, w_gate, w_up, w_down)


def _combine_kernel(wt_ref, base_ref, mod_ref, ys_ref, o_ref):
    w_tok = wt_ref[...].T
    acc_lo = acc_hi = None
    for kk in range(TOP_K):
        lo, hi = _unpack_row_halves(ys_ref[kk])
        w = w_tok[:, kk:kk + 1]
        acc_lo = w * lo if kk == 0 else acc_lo + w * lo
        acc_hi = w * hi if kk == 0 else acc_hi + w * hi
    o_ref[:, :PACKED] = base_ref[:, :PACKED] + mod_ref[0, 5:6, :PACKED] * acc_lo
    o_ref[:, PACKED:] = base_ref[:, PACKED:] + mod_ref[0, 5:6, PACKED:] * acc_hi


def _combine(w_t, base, mod3, ys):
    tm = COMB_TM
    rows_per_batch = SEQ // tm
    return pl.pallas_call(
        _combine_kernel,
        grid=(N_TOK // tm,),
        in_specs=[
            pl.BlockSpec((TOP_K, tm), lambda i: (0, i)),
            pl.BlockSpec((tm, D_MODEL), lambda i: (i, 0)),
            pl.BlockSpec((1, 6, D_MODEL), lambda i: (i // rows_per_batch, 0, 0)),
            pl.BlockSpec((TOP_K, tm, PACKED), lambda i: (0, i, 0)),
        ],
        out_specs=pl.BlockSpec((tm, D_MODEL), lambda i: (i, 0)),
        out_shape=jax.ShapeDtypeStruct((N_TOK, D_MODEL), F32),
        compiler_params=_params(("arbitrary",)),
        name="combine",
    )(w_t, base, mod3, ys.reshape(TOP_K, N_TOK, PACKED))


def _work_items(counts):
    rows = MOE_ROWS
    grp_end = jnp.cumsum(counts)
    grp_start = grp_end - counts
    first_blk = grp_start // rows
    last_blk = (grp_end - 1) // rows
    n_blk = jnp.where(counts > 0, last_blk - first_blk + 1, 0)
    item_end = jnp.cumsum(n_blk)
    item_start = item_end - n_blk
    n_items = item_end[-1]

    ids = jnp.arange(N_EXPERTS, dtype=I32)
    nonempty = (n_blk > 0).astype(I32)
    ordinal = jnp.cumsum(nonempty) - nonempty
    later = lax.cummin(jnp.where(nonempty > 0, ids, N_EXPERTS), axis=0, reverse=True)
    nxt_e = jnp.concatenate([later[1:], jnp.full((1,), N_EXPERTS, I32)])
    nxt_e = jnp.where(nxt_e >= N_EXPERTS, -1, nxt_e)
    nxt2_e = jnp.where(nxt_e >= 0, jnp.take(nxt_e, jnp.maximum(nxt_e, 0)), -1)

    i = jnp.minimum(jnp.arange(MAX_ITEMS, dtype=I32), n_items - 1)
    own = (item_start[None, :] <= i[:, None]) & (i[:, None] < item_end[None, :])
    pick = lambda v: jnp.sum(jnp.where(own, v[None, :], 0), axis=1).astype(I32)
    blk = pick(first_blk - item_start) + i
    lo = jnp.maximum(pick(grp_start), blk * rows) - blk * rows
    hi = jnp.minimum(pick(grp_end), (blk + 1) * rows) - blk * rows
    first = (pick(item_start) == i).astype(I32)
    items = (blk, pick(ids), lo, hi, first, pick(ordinal) % WEIGHT_SLOTS, pick(nxt_e), pick(nxt2_e),
             n_items.astype(I32).reshape(1))
    return grp_start.astype(I32), items


def kernel(x, c, w_ada, b_ada, norm1_g, w_in, q_norm_g, k_norm_g, conv_w, attn_out_g, conv_out_g,
           w_o, norm2_g, w_router, router_bias, w_gate, w_up, w_down, ws_gate, ws_up, ws_down):
    b, s, d = x.shape
    x2 = x.reshape(b * s, d)

    c8 = jnp.pad(c, ((0, 8 - b), (0, 0)))
    mod3 = _adaln(c8, w_ada, b_ada)[:b].reshape(b, 6, d)

    qk_gain = jnp.concatenate([jnp.tile(q_norm_g[0] * HEAD_DIM ** -0.5, ATTN_HEADS),
                               jnp.tile(k_norm_g[0], ATTN_HEADS)]).reshape(1, 2 * ATTN_WIDTH)
    proj = _in_proj(x2, mod3, norm1_g, w_in, qk_gain)

    a, y = _mixer(proj, conv_w, attn_out_g, conv_out_g)
    x1 = _out_proj(a, y, w_o, x2, mod3)

    w_rt = w_router[0].T
    bias_col = jnp.broadcast_to(router_bias[0][:, None], (N_EXPERTS, LANES))
    idx_t, w_t, pos_t, cnt = _route(x1, norm2_g, mod3, w_rt, bias_col)

    grp_start, items = _work_items(cnt[:, 0])
    start_col = jnp.broadcast_to(grp_start.astype(F32)[:, None], (N_EXPERTS, LANES))
    slot_tok, slot_dst = _invperm(start_col, idx_t, pos_t)
    hf, base = _shared(x1, norm2_g, mod3, ws_gate, ws_up, ws_down)
    ys = _experts(items, slot_tok, slot_dst, hf, w_gate, w_up, w_down)
    out = _combine(w_t, base, mod3, ys)
    return out.reshape(b, s, d)
```
